```python
import math, functools
import jax, jax.numpy as jnp
from jax import lax
import numpy as np

D_MODEL = 2048
BATCH = 1
SEQ = 16384
DEPTH = 1
DEC_BATCH = 32
DEC_SEQ = 8
PAST_LEN = 16384
PAGE_SIZE = 128

EPS = 1e-6
NEG_INF = -1e30
CHUNK = 128
SGU_WIDTH = D_MODEL // 2
SGU_GROUPS = 8
SGU_GROUP_DIM = SGU_WIDTH // SGU_GROUPS
HEAD_DIM = 128
KV_HEADS = 8
DIL_PAIRS = ((128, 1), (512, 4), (2048, 16))
N_DIL = 3
Q_HEADS = N_DIL * KV_HEADS
MAX_WINDOW = 2048
KV_WIDTH = KV_HEADS * HEAD_DIM
ATT_WIDTH = KV_HEADS * HEAD_DIM
REL_BUCKETS = 32
REL_MAX_DIST = 2048
IN_WIDTH = 2 * SGU_WIDTH + Q_HEADS * HEAD_DIM + 2 * KV_WIDTH + SGU_WIDTH + ATT_WIDTH
MEM_TOKENS = 256
MEM_HEADS = 4
MEM_HEAD_DIM = D_MODEL // MEM_HEADS
MEM_WIDTH = MEM_HEADS * MEM_HEAD_DIM
PEER_HEADS = 8
PEER_N_KEYS = 128
PEER_KEY_DIM = 128
PEER_TOPK = 16
PEER_BLOCK = 128

kernel_name = "gated_gmlp_dilated_attn_peer_step"


def _rmsnorm(x, g):
    xf = x.astype(jnp.float32)
    y = xf * lax.rsqrt(jnp.mean(xf * xf, axis=-1, keepdims=True) + EPS)
    return (y * g.astype(jnp.float32)).astype(x.dtype)


def _layernorm(x, g, b):
    xf = x.astype(jnp.float32)
    mu = jnp.mean(xf, axis=-1, keepdims=True)
    var = jnp.mean(jnp.square(xf - mu), axis=-1, keepdims=True)
    y = (xf - mu) * lax.rsqrt(var + EPS) * g.astype(jnp.float32) + b.astype(jnp.float32)
    return y.astype(x.dtype)


def _t5_bucket(dist):
    max_exact = REL_BUCKETS // 2
    d = np.maximum(dist, 1).astype(np.float64)
    large = max_exact + (np.log(d / max_exact) / math.log(REL_MAX_DIST / max_exact)
                         * (REL_BUCKETS - max_exact)).astype(np.int64)
    large = np.minimum(large, REL_BUCKETS - 1)
    return np.where(dist < max_exact, dist, large).astype(np.int32)


def _chunk_mix(v, w_s, b_s):
    n = v.shape[2]
    w = jnp.tril(w_s[:, :n, :n])
    return jnp.einsum('gij,bcjgd->bcigd', w, v) + b_s[:, :n].T[None, None, :, :, None]


def _dilated_prompt(q, k, v, bias, dil, steps):
    B, S, H, Dh = q.shape
    span = dil * steps
    L = -(-S // span) * span
    pad = ((0, 0), (0, L - S), (0, 0), (0, 0))
    q, k, v = jnp.pad(q, pad), jnp.pad(k, pad), jnp.pad(v, pad)
    nb = L // span
    qb = q.reshape(B, nb, steps, dil, H, Dh)

    def with_prev(t):
        t = t.reshape(B, nb, steps, dil, H, Dh)
        prev = jnp.pad(t[:, :-1], ((0, 0), (1, 0), (0, 0), (0, 0), (0, 0), (0, 0)))
        return jnp.concatenate([prev, t], axis=2)

    kc, vc = with_prev(k), with_prev(v)
    logits = jnp.einsum('bnirhd,bnjrhd->bnrhij', qb, kc,
                        preferred_element_type=jnp.float32) * (HEAD_DIM ** -0.5)
    i = np.arange(steps)[:, None]
    j = np.arange(2 * steps)[None, :]
    off = i + steps - j
    band = (off >= 0) & (off <= steps)
    blk = np.arange(nb)[:, None, None]
    valid = band[None] & ((blk > 0) | (j >= steps)[None])
    bias_b = bias.astype(jnp.float32)[:, np.clip(off, 0, steps)]
    logits = jnp.where(valid[None, :, None, None], logits + bias_b[None, None, None], NEG_INF)
    lse = jax.nn.logsumexp(logits, axis=-1)
    p = jnp.exp(logits - lse[..., None])
    o = jnp.einsum('bnrhij,bnjrhd->bnirhd', p.astype(v.dtype), vc)
    o = o.reshape(B, L, H, Dh)[:, :S]
    lse = lse.transpose(0, 1, 4, 2, 3).reshape(B, L, H)[:, :S]
    return o, lse


def _dilated_sample(q, kc, vc, bias, dil, steps, n_past):
    T = q.shape[1]
    idx = n_past + np.arange(T)[:, None] - dil * np.arange(steps + 1)[None, :]
    valid = idx >= 0
    idx = np.maximum(idx, 0)
    kg = kc[:, idx]
    vg = vc[:, idx]
    logits = jnp.einsum('bthd,btkhd->bhtk', q, kg, preferred_element_type=jnp.float32) * (HEAD_DIM ** -0.5)
    logits = jnp.where(valid[None, None], logits + bias.astype(jnp.float32)[:, None, :], NEG_INF)
    lse = jax.nn.logsumexp(logits, axis=-1)
    p = jnp.exp(logits - lse[..., None])
    o = jnp.einsum('bhtk,btkhd->bthd', p.astype(vc.dtype), vg)
    return o, lse.transpose(0, 2, 1)


def _dilated_attention(q, k, v, rel_bias, attend):
    outs, lses = [], []
    for g, (win, dil) in enumerate(DIL_PAIRS):
        steps = win // dil
        bucket = _t5_bucket(dil * np.arange(steps + 1))
        bias = rel_bias[bucket, g * KV_HEADS:(g + 1) * KV_HEADS].T
        o, l = attend(q[:, :, g], k, v, bias, dil, steps)
        outs.append(o)
        lses.append(l)
    w = jax.nn.softmax(jnp.stack(lses), axis=0)
    o = jnp.einsum('gbsh,gbshd->bshd', w.astype(v.dtype), jnp.stack(outs))
    B, S = q.shape[:2]
    return o.reshape(B, S, ATT_WIDTH)


def _memory_kv(mem, g, w_mk, w_mv):
    m = _rmsnorm(mem, g)
    B, M, _ = mem.shape
    mk = (m @ w_mk).reshape(B, M, MEM_HEADS, MEM_HEAD_DIM)
    mv = (m @ w_mv).reshape(B, M, MEM_HEADS, MEM_HEAD_DIM)
    return jnp.stack([mk, mv], axis=2)


def _memory_attention(h, mem_kv, w_mq, w_mo):
    B, S, _ = h.shape
    q = (h @ w_mq).reshape(B, S, MEM_HEADS, MEM_HEAD_DIM)
    logits = jnp.einsum('bshd,bmhd->bhsm', q, mem_kv[:, :, 0],
                        preferred_element_type=jnp.float32) * (MEM_HEAD_DIM ** -0.5)
    p = jax.nn.softmax(logits, axis=-1)
    o = jnp.einsum('bhsm,bmhd->bshd', p.astype(h.dtype), mem_kv[:, :, 1])
    return o.reshape(B, S, MEM_WIDTH) @ w_mo


def _peer(h, w_pq, keys1, keys2, u_tab, v_tab):
    B, S, D = h.shape
    n = B * S
    npad = -(-n // PEER_BLOCK) * PEER_BLOCK
    xb = jnp.pad(h.reshape(n, D), ((0, npad - n), (0, 0))).reshape(npad // PEER_BLOCK, PEER_BLOCK, D)

    def block(xt):
        q = (xt @ w_pq).reshape(PEER_BLOCK, PEER_HEADS, 2, PEER_KEY_DIM)
        s1 = jnp.einsum('thd,hkd->thk', q[:, :, 0], keys1, preferred_element_type=jnp.float32)
        s2 = jnp.einsum('thd,hkd->thk', q[:, :, 1], keys2, preferred_element_type=jnp.float32)
        v1, i1 = lax.top_k(s1, PEER_TOPK)
        v2, i2 = lax.top_k(s2, PEER_TOPK)
        cand = (v1[..., :, None] + v2[..., None, :]).reshape(PEER_BLOCK, PEER_HEADS, PEER_TOPK * PEER_TOPK)
        cidx = (i1[..., :, None] * PEER_N_KEYS + i2[..., None, :]).reshape(PEER_BLOCK, PEER_HEADS, PEER_TOPK * PEER_TOPK)
        best, pos = lax.top_k(cand, PEER_TOPK)
        eidx = jnp.take_along_axis(cidx, pos, axis=-1)
        gate = jax.nn.softmax(best, axis=-1)
        ue = u_tab[eidx]
        ve = v_tab[eidx]
        a = jax.nn.gelu(jnp.einsum('td,thkd->thk', xt, ue, preferred_element_type=jnp.float32))
        return jnp.einsum('thk,thkd->td', (gate * a).astype(xt.dtype), ve)

    y = lax.map(block, xb).reshape(npad, D)[:n]
    return y.reshape(B, S, D)


def _layer(x, mem_kv, cache_win, lw, rel_bias):
    B, S, _ = x.shape
    h = _rmsnorm(x, lw['norm_mix'])
    z = h @ lw['w_in']
    splits = np.cumsum([SGU_WIDTH, SGU_WIDTH, Q_HEADS * HEAD_DIM, KV_WIDTH, KV_WIDTH, SGU_WIDTH]).tolist()
    u, vs, q, k, v, ga, gb = jnp.split(z, splits, axis=-1)
    u = jax.nn.gelu(u)
    vs = _layernorm(jax.nn.gelu(vs), lw['sgu_ln_g'], lw['sgu_ln_b'])
    n = min(S, CHUNK)
    vch = vs.reshape(B, S // n, n, SGU_GROUPS, SGU_GROUP_DIM)
    ya = u * _chunk_mix(vch, lw['sgu_w'], lw['sgu_b']).reshape(B, S, SGU_WIDTH)
    q = q.reshape(B, S, N_DIL, KV_HEADS, HEAD_DIM)
    k = k.reshape(B, S, KV_HEADS, HEAD_DIM)
    v = v.reshape(B, S, KV_HEADS, HEAD_DIM)
    new_kv = jnp.stack([k, v], axis=2)
    if cache_win is None:
        yb = _dilated_attention(q, k, v, rel_bias, _dilated_prompt)
        win_state = new_kv[:, -min(MAX_WINDOW, S):]
    else:
        kc = jnp.concatenate([cache_win[:, :, 0], k], axis=1)
        vc = jnp.concatenate([cache_win[:, :, 1], v], axis=1)
        attend = functools.partial(_dilated_sample, n_past=cache_win.shape[1])
        yb = _dilated_attention(q, kc, vc, rel_bias, attend)
        win_state = new_kv
    merged = jnp.concatenate([jax.nn.sigmoid(ga) * ya, jax.nn.sigmoid(gb) * yb], axis=-1)
    x = x + merged @ lw['w_out']
    x = x + _memory_attention(_rmsnorm(x, lw['norm_mem']), mem_kv, lw['w_mq'], lw['w_mo'])
    h = _rmsnorm(x, lw['norm_peer'])
    x = x + _peer(h, lw['peer_wq'], lw['peer_keys1'], lw['peer_keys2'], lw['peer_u'], lw['peer_v'])
    return x, win_state, vs


def setup_inputs(seed: int = 0) -> dict:
    key = jax.random.key(seed)
    ks = jax.random.split(key, 32)

    def nrm(k, shape, scale):
        return jax.random.normal(k, shape, jnp.float32) * scale

    def gain(k, shape):
        return 1.0 + nrm(k, shape, 0.02)

    win_rows = min(MAX_WINDOW, PAST_LEN)
    return {
        'x_prompt': nrm(ks[0], (BATCH, SEQ, D_MODEL), 1.0),
        'x_sample': nrm(ks[1], (DEC_BATCH, DEC_SEQ, D_MODEL), 1.0),
        'mem_prompt': nrm(ks[2], (BATCH, MEM_TOKENS, D_MODEL), 1.0),
        'cache_win': nrm(ks[3], (DEPTH, DEC_BATCH, win_rows, 2, KV_HEADS, HEAD_DIM), 1.0),
        'cache_mem_kv': nrm(ks[4], (DEPTH, DEC_BATCH, MEM_TOKENS, 2, MEM_HEADS, MEM_HEAD_DIM), 1.0),
        'rel_bias': nrm(ks[5], (REL_BUCKETS, Q_HEADS), 0.5),
        'norm_mix': gain(ks[6], (DEPTH, D_MODEL)),
        'w_in': nrm(ks[7], (DEPTH, D_MODEL, IN_WIDTH), D_MODEL ** -0.5),
        'sgu_ln_g': gain(ks[8], (DEPTH, SGU_WIDTH)),
        'sgu_ln_b': nrm(ks[9], (DEPTH, SGU_WIDTH), 0.02),
        'sgu_w': nrm(ks[10], (DEPTH, SGU_GROUPS, CHUNK, CHUNK), CHUNK ** -0.5),
        'sgu_b': 1.0 + nrm(ks[11], (DEPTH, SGU_GROUPS, CHUNK), 0.02),
        'w_out': nrm(ks[12], (DEPTH, SGU_WIDTH + ATT_WIDTH, D_MODEL), (SGU_WIDTH + ATT_WIDTH) ** -0.5),
        'norm_mem': gain(ks[13], (DEPTH, D_MODEL)),
        'norm_memtok': gain(ks[14], (DEPTH, D_MODEL)),
        'w_mq': nrm(ks[15], (DEPTH, D_MODEL, MEM_WIDTH), D_MODEL ** -0.5),
        'w_mk': nrm(ks[16], (DEPTH, D_MODEL, MEM_WIDTH), D_MODEL ** -0.5),
        'w_mv': nrm(ks[17], (DEPTH, D_MODEL, MEM_WIDTH), D_MODEL ** -0.5),
        'w_mo': nrm(ks[18], (DEPTH, MEM_WIDTH, D_MODEL), MEM_WIDTH ** -0.5),
        'norm_peer': gain(ks[19], (DEPTH, D_MODEL)),
        'peer_wq': nrm(ks[20], (DEPTH, D_MODEL, PEER_HEADS * 2 * PEER_KEY_DIM), D_MODEL ** -0.5),
        'peer_keys1': nrm(ks[21], (DEPTH, PEER_HEADS, PEER_N_KEYS, PEER_KEY_DIM), PEER_KEY_DIM ** -0.5),
        'peer_keys2': nrm(ks[22], (DEPTH, PEER_HEADS, PEER_N_KEYS, PEER_KEY_DIM), PEER_KEY_DIM ** -0.5),
        'peer_u': nrm(ks[23], (DEPTH, PEER_N_KEYS * PEER_N_KEYS, D_MODEL), D_MODEL ** -0.5),
        'peer_v': nrm(ks[24], (DEPTH, PEER_N_KEYS * PEER_N_KEYS, D_MODEL), 0.3),
        'norm_final': gain(ks[25], (D_MODEL,)),
    }


def reference(x_prompt, x_sample, mem_prompt, cache_win, cache_mem_kv, rel_bias, norm_mix, w_in,
              sgu_ln_g, sgu_ln_b, sgu_w, sgu_b, w_out, norm_mem, norm_memtok, w_mq, w_mk, w_mv, w_mo,
              norm_peer, peer_wq, peer_keys1, peer_keys2, peer_u, peer_v, norm_final):
    yp, ys = x_prompt, x_sample
    win_p, mem_p, win_s, sgu_s = [], [], [], []
    for l in range(DEPTH):
        lw = dict(norm_mix=norm_mix[l], w_in=w_in[l], sgu_ln_g=sgu_ln_g[l], sgu_ln_b=sgu_ln_b[l],
                  sgu_w=sgu_w[l], sgu_b=sgu_b[l], w_out=w_out[l], norm_mem=norm_mem[l],
                  w_mq=w_mq[l], w_mo=w_mo[l], norm_peer=norm_peer[l], peer_wq=peer_wq[l],
                  peer_keys1=peer_keys1[l], peer_keys2=peer_keys2[l], peer_u=peer_u[l], peer_v=peer_v[l])
        mkv = _memory_kv(mem_prompt, norm_memtok[l], w_mk[l], w_mv[l])
        yp, wp, _ = _layer(yp, mkv, None, lw, rel_bias)
        ys, wsmp, vsmp = _layer(ys, cache_mem_kv[l], cache_win[l], lw, rel_bias)
        win_p.append(wp)
        mem_p.append(mkv)
        win_s.append(wsmp)
        sgu_s.append(vsmp)
    y_prompt = _rmsnorm(yp, norm_final)
    y_sample = _rmsnorm(ys, norm_final)
    return (y_prompt, y_sample, jnp.stack(win_p), jnp.stack(mem_p), jnp.stack(win_s), jnp.stack(sgu_s))
```

```python
import functools
import math

import numpy as np
import jax
import jax.numpy as jnp
from jax import lax
from jax.experimental import pallas as pl
from jax.experimental.pallas import tpu as pltpu

F32 = jnp.float32
BF16 = jnp.bfloat16

EPS = 1e-6
NEG_INF = -1e30

LANES = 128
SUBLANES = 8
VMEM_PHYSICAL_BYTES = 64 * 1024 * 1024
VMEM_COMPILER_RESERVE_BYTES = 6 * 1024 * 1024

CHUNK = 128
SGU_GROUPS = 8
HEAD_DIM = 128
KV_HEADS = 8
DIL_PAIRS = ((128, 1), (512, 4), (2048, 16))
N_DIL = len(DIL_PAIRS)
STEPS = 128
REL_BUCKETS = 32
REL_MAX_DIST = 2048
MEM_HEADS = 4
PEER_HEADS = 8
PEER_N_KEYS = 128
PEER_KEY_DIM = 128
PEER_TOPK = 16

PROMPT_TM = 1024
NORM_TM = 512
PEER_TOKEN_TILE = 512


def _vmem_limit(block_bytes):
    return int(min(block_bytes + VMEM_COMPILER_RESERVE_BYTES, VMEM_PHYSICAL_BYTES - 4 * 1024 * 1024))


def _nbytes(shape, dtype):
    return int(np.prod(shape)) * jnp.dtype(dtype).itemsize


def _params(semantics, block_bytes):
    return pltpu.CompilerParams(dimension_semantics=semantics, vmem_limit_bytes=_vmem_limit(block_bytes))


def _rmsnorm_kernel(x_ref, g_ref, o_ref, *, transpose):
    x = x_ref[...].astype(F32)
    y = x * lax.rsqrt(jnp.mean(x * x, axis=-1, keepdims=True) + EPS) * g_ref[...]
    if transpose:
        y = y.T
    o_ref[...] = y.astype(o_ref.dtype)


def _rmsnorm(x, g, *, tm, out_dtype, transpose=False, name):
    t, d = x.shape
    assert t % tm == 0
    if transpose:
        out_shape = jax.ShapeDtypeStruct((d, t), out_dtype)
        out_spec = pl.BlockSpec((d, tm), lambda i: (0, i))
    else:
        out_shape = jax.ShapeDtypeStruct((t, d), out_dtype)
        out_spec = pl.BlockSpec((tm, d), lambda i: (i, 0))
    blocks = 2 * _nbytes((tm, d), x.dtype) + 2 * _nbytes((tm, d), out_dtype) + 2 * _nbytes((tm, d), F32)
    return pl.pallas_call(
        functools.partial(_rmsnorm_kernel, transpose=transpose),
        grid=(t // tm,),
        in_specs=[pl.BlockSpec((tm, d), lambda i: (i, 0)), pl.BlockSpec((1, d), lambda i: (0, 0))],
        out_specs=out_spec,
        out_shape=out_shape,
        compiler_params=_params(("parallel",), blocks),
        name=name,
    )(x, g.reshape(1, d).astype(F32))


def _epi_cast(acc):
    return acc


def _epi_gelu(acc):
    return jax.nn.gelu(acc)


def _epi_sigmoid(acc):
    return jax.nn.sigmoid(acc)


def _epi_gelu_layernorm(acc, g, b):
    a = jax.nn.gelu(acc)
    mu = jnp.mean(a, axis=-1, keepdims=True)
    var = jnp.mean(jnp.square(a - mu), axis=-1, keepdims=True)
    return (a - mu) * lax.rsqrt(var + EPS) * g + b


def _epi_residual(acc, res):
    return res + acc


def _matmul_kernel(*refs, n_pairs, n_extras, epilogue):
    out_ref = refs[-1]
    acc = None
    for p in range(n_pairs):
        part = jnp.dot(refs[2 * p][...].astype(BF16), refs[2 * p + 1][...], preferred_element_type=F32)
        acc = part if acc is None else acc + part
    extras = [refs[2 * n_pairs + e][...] for e in range(n_extras)]
    out_ref[...] = epilogue(acc, *extras).astype(out_ref.dtype)


def _matmul(xs, ws, *, tm, tn, out_dtype, epilogue=_epi_cast, extras=(), name):
    m = xs[0].shape[0]
    n = ws[0].shape[1]
    assert m % tm == 0 and n % tn == 0, (m, tm, n, tn)
    in_specs, args, blocks = [], [], 0
    for x, w in zip(xs, ws):
        k = x.shape[1]
        assert w.shape == (k, n) and x.shape == (m, k)
        in_specs += [pl.BlockSpec((tm, k), lambda i, j: (i, 0)), pl.BlockSpec((k, tn), lambda i, j: (0, j))]
        args += [x, w]
        blocks += 2 * _nbytes((tm, k), x.dtype) + 2 * _nbytes((k, tn), w.dtype)
    for kind, arr in extras:
        if kind == "row":
            assert arr.shape == (1, n)
            in_specs.append(pl.BlockSpec((1, tn), lambda i, j: (0, j)))
        else:
            assert kind == "full" and arr.shape == (m, n)
            in_specs.append(pl.BlockSpec((tm, tn), lambda i, j: (i, j)))
            blocks += 2 * _nbytes((tm, tn), arr.dtype)
        args.append(arr)
    blocks += 2 * _nbytes((tm, tn), out_dtype) + 2 * _nbytes((tm, tn), F32)
    return pl.pallas_call(
        functools.partial(_matmul_kernel, n_pairs=len(xs), n_extras=len(extras), epilogue=epilogue),
        grid=(m // tm, n // tn),
        in_specs=in_specs,
        out_specs=pl.BlockSpec((tm, tn), lambda i, j: (i, j)),
        out_shape=jax.ShapeDtypeStruct((m, n), out_dtype),
        compiler_params=_params(("parallel", "parallel"), blocks),
        name=name,
    )(*args)


def _sgu_kernel(u_ref, vs_ref, gate_ref, w_ref, mask_ref, b_ref, o_ref, *, n_chunks):
    mask = mask_ref[...]
    for g in range(SGU_GROUPS):
        cols = slice(g * CHUNK, (g + 1) * CHUNK)
        wm = (w_ref[g] * mask).astype(BF16)
        for c in range(n_chunks):
            rows = slice(c * CHUNK, (c + 1) * CHUNK)
            mixed = jnp.dot(wm, vs_ref[rows, cols].astype(BF16), preferred_element_type=F32) + b_ref[:, cols]
            ya = u_ref[rows, cols].astype(F32) * mixed
            o_ref[rows, cols] = (gate_ref[rows, cols].astype(F32) * ya).astype(o_ref.dtype)


def _sgu(u, vs, gates, w_mix, mask, bias_full, *, n_chunks, name):
    t, width = u.shape
    rows = n_chunks * CHUNK
    assert t % rows == 0 and width == SGU_GROUPS * CHUNK
    blocks = (2 * _nbytes((rows, width), u.dtype) + 2 * _nbytes((rows, width), vs.dtype)
              + 2 * _nbytes((rows, width), gates.dtype) + 2 * _nbytes((rows, width), BF16)
              + 2 * _nbytes(w_mix.shape, F32) + 4 * _nbytes((CHUNK, width), F32))
    return pl.pallas_call(
        functools.partial(_sgu_kernel, n_chunks=n_chunks),
        grid=(t // rows,),
        in_specs=[
            pl.BlockSpec((rows, width), lambda i: (i, 0)),
            pl.BlockSpec((rows, width), lambda i: (i, 0)),
            pl.BlockSpec((rows, width), lambda i: (i, 0)),
            pl.BlockSpec((SGU_GROUPS, CHUNK, CHUNK), lambda i: (0, 0, 0)),
            pl.BlockSpec((CHUNK, CHUNK), lambda i: (0, 0)),
            pl.BlockSpec((CHUNK, width), lambda i: (0, 0)),
        ],
        out_specs=pl.BlockSpec((rows, width), lambda i: (i, 0)),
        out_shape=jax.ShapeDtypeStruct((t, width), BF16),
        compiler_params=_params(("parallel",), blocks),
        name=name,
    )(u, vs, gates, w_mix, mask, bias_full)


def _dil_attn_kernel(q_ref, kc_ref, vc_ref, kp_ref, vp_ref, bias_ref, o_ref, lse_ref, *, n_sub):
    first = pl.program_id(1) == 0
    no_prev = jnp.where(first, NEG_INF, 0.0).astype(F32)
    scale = HEAD_DIM ** -0.5
    nt = (((1,), (1,)), ((), ()))
    for sb in range(n_sub):
        r0 = sb * STEPS
        rows = slice(r0, r0 + STEPS)
        for h in range(KV_HEADS):
            cs = slice(h * HEAD_DIM, (h + 1) * HEAD_DIM)
            q = q_ref[rows, cs]
            if sb == 0:
                ka = kp_ref[:, cs].astype(BF16)
                va = vp_ref[:, cs].astype(BF16)
            else:
                ka = kc_ref[r0 - STEPS:r0, cs].astype(BF16)
                va = vc_ref[r0 - STEPS:r0, cs].astype(BF16)
            kb = kc_ref[rows, cs].astype(BF16)
            vb = vc_ref[rows, cs].astype(BF16)
            la = lax.dot_general(q, ka, nt, preferred_element_type=F32) * scale + bias_ref[h, :, 0:STEPS]
            if sb == 0:
                la = la + no_prev
            lb = lax.dot_general(q, kb, nt, preferred_element_type=F32) * scale + bias_ref[h, :, STEPS:2 * STEPS]
            m = jnp.maximum(jnp.max(la, axis=-1, keepdims=True), jnp.max(lb, axis=-1, keepdims=True))
            ea = jnp.exp(la - m)
            eb = jnp.exp(lb - m)
            s = jnp.sum(ea, axis=-1, keepdims=True) + jnp.sum(eb, axis=-1, keepdims=True)
            o = (jnp.dot(ea.astype(BF16), va, preferred_element_type=F32)
                 + jnp.dot(eb.astype(BF16), vb, preferred_element_type=F32))
            o_ref[rows, cs] = (o / s).astype(o_ref.dtype)
            lse_ref[rows, cs] = jnp.broadcast_to(m + jnp.log(s), (STEPS, HEAD_DIM))


def _dil_attn(q, kv, bias_tab, *, group, dil, n_sub, name):
    s, _ = q.shape
    w = KV_HEADS * HEAD_DIM
    sd = s // dil
    n_sub = math.gcd(n_sub, sd // STEPS)
    tq = n_sub * STEPS
    assert s % (dil * tq) == 0
    q2 = q.reshape(sd, dil * N_DIL * w)
    kv2 = kv.reshape(sd, dil * 2 * w)
    blocks = (2 * _nbytes((tq, w), BF16) + 4 * _nbytes((tq, w), F32) + 4 * _nbytes((STEPS, w), F32)
              + 2 * _nbytes(bias_tab.shape, F32) + 2 * _nbytes((tq, w), BF16) + 2 * _nbytes((tq, w), F32))
    o, lse = pl.pallas_call(
        functools.partial(_dil_attn_kernel, n_sub=n_sub),
        grid=(dil, sd // tq),
        in_specs=[
            pl.BlockSpec((tq, w), lambda r, n: (n, r * N_DIL + group)),
            pl.BlockSpec((tq, w), lambda r, n: (n, r * 2)),
            pl.BlockSpec((tq, w), lambda r, n: (n, r * 2 + 1)),
            pl.BlockSpec((STEPS, w), lambda r, n: (jnp.maximum(n * n_sub - 1, 0), r * 2)),
            pl.BlockSpec((STEPS, w), lambda r, n: (jnp.maximum(n * n_sub - 1, 0), r * 2 + 1)),
            pl.BlockSpec((KV_HEADS, STEPS, 2 * STEPS), lambda r, n: (0, 0, 0)),
        ],
        out_specs=[pl.BlockSpec((tq, w), lambda r, n: (n, r)), pl.BlockSpec((tq, w), lambda r, n: (n, r))],
        out_shape=[jax.ShapeDtypeStruct((sd, dil * w), BF16), jax.ShapeDtypeStruct((sd, dil * w), F32)],
        compiler_params=_params(("parallel", "parallel"), blocks),
        name=name,
    )(q2, kv2, kv2, kv2, kv2, bias_tab)
    return o.reshape(s, w), lse.reshape(s, w)


def _combine_kernel(o0_ref, o1_ref, o2_ref, l0_ref, l1_ref, l2_ref, gate_ref, out_ref):
    l0, l1, l2 = l0_ref[...], l1_ref[...], l2_ref[...]
    mx = jnp.maximum(jnp.maximum(l0, l1), l2)
    w0, w1, w2 = jnp.exp(l0 - mx), jnp.exp(l1 - mx), jnp.exp(l2 - mx)
    mix = w0 * o0_ref[...].astype(F32) + w1 * o1_ref[...].astype(F32) + w2 * o2_ref[...].astype(F32)
    out_ref[...] = (gate_ref[...].astype(F32) * (mix / (w0 + w1 + w2))).astype(out_ref.dtype)


def _combine(outs, lses, gates, *, tm, name):
    t, w = outs[0].shape
    assert t % tm == 0
    blk = lambda: pl.BlockSpec((tm, w), lambda i: (i, 0))
    blocks = 2 * (4 * _nbytes((tm, w), BF16) + 3 * _nbytes((tm, w), F32)) + 8 * _nbytes((tm, w), F32)
    return pl.pallas_call(
        _combine_kernel,
        grid=(t // tm,),
        in_specs=[blk(), blk(), blk(), blk(), blk(), blk(), pl.BlockSpec((tm, w), lambda i: (i, 1))],
        out_specs=blk(),
        out_shape=jax.ShapeDtypeStruct((t, w), BF16),
        compiler_params=_params(("parallel",), blocks),
        name=name,
    )(*outs, *lses, gates)


def _smp_attn_kernel(q0_ref, q1_ref, q2_ref, kn_ref, vn_ref, ck_ref, cv_ref, tab_ref, gate_ref, o_ref,
                     kbuf, vbuf, *, n_past, n_new):
    n_keys = kbuf.shape[0]
    kbuf[0:n_past, :] = ck_ref[...]
    vbuf[0:n_past, :] = cv_ref[...]
    kbuf[n_past:n_past + n_new, :] = kn_ref[...]
    vbuf[n_past:n_past + n_new, :] = vn_ref[...]
    pad = n_keys - n_past - n_new
    kbuf[n_past + n_new:, :] = jnp.zeros((pad, HEAD_DIM), F32)
    vbuf[n_past + n_new:, :] = jnp.zeros((pad, HEAD_DIM), F32)
    q = jnp.concatenate([q0_ref[...].astype(F32), q1_ref[...].astype(F32), q2_ref[...].astype(F32),
                         jnp.zeros((n_new, HEAD_DIM), F32)], axis=0)
    logits = lax.dot_general(q.astype(BF16), kbuf[...].astype(BF16), (((1,), (1,)), ((), ())),
                             preferred_element_type=F32) * (HEAD_DIM ** -0.5) + tab_ref[...]
    m = jnp.max(logits, axis=-1, keepdims=True)
    e = jnp.exp(logits - m)
    s = jnp.sum(e, axis=-1, keepdims=True)
    o = jnp.dot(e.astype(BF16), vbuf[...].astype(BF16), preferred_element_type=F32) / s
    lse = m + jnp.log(s)
    ls = [lse[g * n_new:(g + 1) * n_new] for g in range(N_DIL)]
    mx = jnp.maximum(jnp.maximum(ls[0], ls[1]), ls[2])
    ws = [jnp.exp(l - mx) for l in ls]
    mix = sum(ws[g] * o[g * n_new:(g + 1) * n_new] for g in range(N_DIL)) / (ws[0] + ws[1] + ws[2])
    o_ref[...] = (gate_ref[...].astype(F32) * mix).astype(o_ref.dtype)


def _smp_attn(q, kv, cache, tab, gates, *, n_new, out_dtype, name):
    b, n_past, _ = cache.shape
    n_keys = tab.shape[-1]
    hb = lambda off: pl.BlockSpec((n_new, HEAD_DIM), lambda i, h, off=off: (i, off + h))
    cb = lambda off: pl.BlockSpec((None, n_past, HEAD_DIM), lambda i, h, off=off: (i, 0, off + h))
    blocks = (4 * _nbytes((n_past, HEAD_DIM), F32) + 2 * _nbytes((n_keys, HEAD_DIM), F32)
              + 2 * _nbytes(tab.shape[1:], F32) + 8 * _nbytes((N_DIL * n_new, n_keys), F32))
    return pl.pallas_call(
        functools.partial(_smp_attn_kernel, n_past=n_past, n_new=n_new),
        grid=(b, KV_HEADS),
        in_specs=[
            hb(0), hb(KV_HEADS), hb(2 * KV_HEADS),
            hb(0), hb(KV_HEADS),
            cb(0), cb(KV_HEADS),
            pl.BlockSpec((None, (N_DIL + 1) * n_new, n_keys), lambda i, h: (h, 0, 0)),
            hb(KV_HEADS),
        ],
        out_specs=pl.BlockSpec((n_new, HEAD_DIM), lambda i, h: (i, h)),
        out_shape=jax.ShapeDtypeStruct((b * n_new, KV_HEADS * HEAD_DIM), out_dtype),
        scratch_shapes=[pltpu.VMEM((n_keys, HEAD_DIM), F32), pltpu.VMEM((n_keys, HEAD_DIM), F32)],
        compiler_params=_params(("parallel", "parallel"), blocks),
        name=name,
    )(q, q, q, kv, kv, cache, cache, tab, gates)


def _mem_attn_kernel(q_ref, k_ref, v_ref, o_ref, *, head_dim):
    scale = head_dim ** -0.5
    for h in range(MEM_HEADS):
        cs = slice(h * head_dim, (h + 1) * head_dim)
        logits = lax.dot_general(q_ref[:, cs].astype(BF16), k_ref[:, cs].astype(BF16), (((1,), (1,)), ((), ())),
                                 preferred_element_type=F32) * scale
        m = jnp.max(logits, axis=-1, keepdims=True)
        e = jnp.exp(logits - m)
        s = jnp.sum(e, axis=-1, keepdims=True)
        o = jnp.dot(e.astype(BF16), v_ref[:, cs].astype(BF16), preferred_element_type=F32)
        o_ref[:, cs] = (o / s).astype(o_ref.dtype)


def _mem_attn(q, mem_kv, *, tm, per_batch, out_dtype, name):
    t, w = q.shape
    m_tok = mem_kv.shape[-2]
    if per_batch:
        kspec = pl.BlockSpec((None, m_tok, w), lambda i: (i, 0, 0))
        vspec = pl.BlockSpec((None, m_tok, w), lambda i: (i, 0, 1))
    else:
        kspec = pl.BlockSpec((m_tok, w), lambda i: (0, 0))
        vspec = pl.BlockSpec((m_tok, w), lambda i: (0, 1))
    blocks = 4 * _nbytes((tm, w), BF16) + 4 * _nbytes((m_tok, w), F32) + 6 * _nbytes((tm, m_tok), F32) \
        + 2 * _nbytes((tm, w), F32)
    return pl.pallas_call(
        functools.partial(_mem_attn_kernel, head_dim=w // MEM_HEADS),
        grid=(t // tm,),
        in_specs=[pl.BlockSpec((tm, w), lambda i: (i, 0)), kspec, vspec],
        out_specs=pl.BlockSpec((tm, w), lambda i: (i, 0)),
        out_shape=jax.ShapeDtypeStruct((t, w), out_dtype),
        compiler_params=_params(("parallel",), blocks),
        name=name,
    )(q, mem_kv, mem_kv)


def _oddeven_merge_sort_pairs(n):
    pairs = []

    def merge(lo, hi, r):
        step = r * 2
        if step < hi - lo:
            merge(lo, hi, step)
            merge(lo + r, hi, step)
            pairs.extend((i, i + r) for i in range(lo + r, hi - r, step))
        else:
            pairs.append((lo, lo + r))

    def sort(lo, hi):
        if hi - lo >= 1:
            mid = lo + (hi - lo) // 2
            sort(lo, mid)
            sort(mid + 1, hi)
            merge(lo, hi, 1)

    sort(0, n - 1)
    return pairs


def _compare_exchange(wires, i, j):
    a, b = wires[i], wires[j]
    if b is None:
        return
    if a is None:
        wires[i], wires[j] = b, None
        return
    wires[i], wires[j] = jnp.maximum(a, b), jnp.minimum(a, b)


def _sort_desc(wires):
    for i, j in _oddeven_merge_sort_pairs(len(wires)):
        _compare_exchange(wires, i, j)
    return wires


def _bitonic_merge_desc(wires):
    n = len(wires)
    dist = n // 2
    while dist >= 1:
        for i in range(n):
            if i & dist == 0:
                _compare_exchange(wires, i, i + dist)
        dist //= 2
    return wires


def _top16_desc(s):
    wires = [s[SUBLANES * i:SUBLANES * (i + 1), :] for i in range(PEER_N_KEYS // SUBLANES)]
    wires = _sort_desc(wires)
    for shift in (4, 2, 1):
        wires = [jnp.maximum(wires[i], pltpu.roll(wires[PEER_TOPK - 1 - i], shift, 0)) for i in range(PEER_TOPK)]
        wires = _bitonic_merge_desc(wires)
    return wires


_CAND_PAIRS = [(j, k) for j in range(PEER_TOPK) for k in range(PEER_TOPK) if (j + 1) * (k + 1) <= PEER_TOPK]


def _peer_scores_kernel(qt_ref, k1_ref, k2_ref, s1_ref, e1_ref, s2_ref, e2_ref, tau_ref, *, n_lane_tiles):
    for h in range(PEER_HEADS):
        r0 = h * 2 * PEER_KEY_DIM
        s1_ref[h] = jnp.dot(k1_ref[h], qt_ref[r0:r0 + PEER_KEY_DIM, :], preferred_element_type=F32)
        s2_ref[h] = jnp.dot(k2_ref[h], qt_ref[r0 + PEER_KEY_DIM:r0 + 2 * PEER_KEY_DIM, :],
                            preferred_element_type=F32)
    sub = lax.broadcasted_iota(jnp.int32, (SUBLANES, LANES), 0)

    def lane_tile(lt, carry):
        lanes = pl.ds(pl.multiple_of(lt * LANES, LANES), LANES)
        a_pack = [None] * PEER_TOPK
        b_pack = [None] * PEER_TOPK
        for h in range(PEER_HEADS):
            ta = _top16_desc(s1_ref[h, :, lanes])
            tb = _top16_desc(s2_ref[h, :, lanes])
            for j in range(PEER_TOPK):
                a_pack[j] = ta[j] if h == 0 else jnp.where(sub == h, ta[j], a_pack[j])
                b_pack[j] = tb[j] if h == 0 else jnp.where(sub == h, tb[j], b_pack[j])
        cands = [a_pack[j] + b_pack[k] for j, k in _CAND_PAIRS]
        top = cands[0]
        wires = _sort_desc(list(cands) + [None] * (64 - len(cands)))
        tau = wires[PEER_TOPK - 1]
        z = jnp.zeros_like(tau)
        for c in cands:
            z = z + jnp.where(c >= tau, jnp.exp(c - top), 0.0)
        zinv = 1.0 / z
        tau_ref[:, lanes] = tau
        for h in range(PEER_HEADS):
            e1_ref[h, :, lanes] = jnp.exp(s1_ref[h, :, lanes] - a_pack[0][h:h + 1, :])
            e2_ref[h, :, lanes] = jnp.exp(s2_ref[h, :, lanes] - b_pack[0][h:h + 1, :]) * zinv[h:h + 1, :]
        return carry

    lax.fori_loop(0, n_lane_tiles, lane_tile, 0)


def _peer_scores(qt, keys1, keys2, *, tl, name):
    d, t = qt.shape
    assert t % tl == 0 and d == PEER_HEADS * 2 * PEER_KEY_DIM
    big = jax.ShapeDtypeStruct((PEER_HEADS, PEER_N_KEYS, t), F32)
    bspec = lambda: pl.BlockSpec((PEER_HEADS, PEER_N_KEYS, tl), lambda i: (0, 0, i))
    blocks = 2 * _nbytes((d, tl), BF16) + 4 * _nbytes(keys1.shape, BF16) + 8 * _nbytes((PEER_HEADS, PEER_N_KEYS, tl), F32)
    return pl.pallas_call(
        functools.partial(_peer_scores_kernel, n_lane_tiles=tl // LANES),
        grid=(t // tl,),
        in_specs=[
            pl.BlockSpec((d, tl), lambda i: (0, i)),
            pl.BlockSpec(keys1.shape, lambda i: (0, 0, 0)),
            pl.BlockSpec(keys2.shape, lambda i: (0, 0, 0)),
        ],
        out_specs=[bspec(), bspec(), bspec(), bspec(), pl.BlockSpec((PEER_HEADS, tl), lambda i: (0, i))],
        out_shape=[big, big, big, big, jax.ShapeDtypeStruct((PEER_HEADS, t), F32)],
        compiler_params=_params(("parallel",), blocks),
        name=name,
    )(qt, keys1, keys2)


def _peer_dense_step(u_ref, vt_ref, ht_ref, s1_ref, e1_ref, s2_ref, e2_ref, tau_ref, o_ref,
                     a_new, a_old, w_new, w_old, key_group, key_row_in_group, rows_per_block, n_lane_tiles):
    a_new[...] = jnp.dot(u_ref[...], ht_ref[...], preferred_element_type=F32)

    o_ref[...] += jnp.dot(vt_ref[...], w_old[...], preferred_element_type=F32)

    group = pl.ds(pl.multiple_of(key_group * SUBLANES, SUBLANES), SUBLANES)
    for lt in range(n_lane_tiles):
        lanes = slice(lt * LANES, (lt + 1) * LANES)
        s1_rows = [s1_ref[h, group, lanes] for h in range(PEER_HEADS)]
        e1_rows = [e1_ref[h, group, lanes] for h in range(PEER_HEADS)]
        for ii in range(rows_per_block):
            r = key_row_in_group + ii
            erows = slice(ii * PEER_N_KEYS, (ii + 1) * PEER_N_KEYS)
            gate = None
            for h in range(PEER_HEADS):
                hit = (s1_rows[h][r:r + 1, :] + s2_ref[h, :, lanes]) >= tau_ref[h:h + 1, lanes]
                term = jnp.where(hit, e1_rows[h][r:r + 1, :] * e2_ref[h, :, lanes], 0.0)
                gate = term if gate is None else gate + term
            act = jax.nn.gelu(a_old[erows, lanes])
            w_new[erows, lanes] = (gate * act).astype(w_new.dtype)


def _peer_dense_kernel(u_ref, vt_ref, ht_ref, s1_ref, e1_ref, s2_ref, e2_ref, tau_ref, o_ref,
                       a0, a1, w0, w1, *, n_blocks, rows_per_block, n_lane_tiles):
    j = pl.program_id(1)

    @pl.when(j == 0)
    def _():
        o_ref[...] = jnp.zeros_like(o_ref)
        a1[...] = jnp.zeros_like(a1)
        w0[...] = jnp.zeros_like(w0)

    blocks_per_group = SUBLANES // rows_per_block
    key_group = jnp.clip(j - 1, 0, n_blocks - 1) // blocks_per_group
    step = functools.partial(_peer_dense_step, u_ref, vt_ref, ht_ref, s1_ref, e1_ref, s2_ref, e2_ref, tau_ref,
                             o_ref, key_group=key_group, rows_per_block=rows_per_block,
                             n_lane_tiles=n_lane_tiles)
    even = lax.rem(j, 2) == 0

    @pl.when(even)
    def _():
        step(a_new=a0, a_old=a1, w_new=w1, w_old=w0, key_row_in_group=(blocks_per_group - 1) * rows_per_block)

    @pl.when(jnp.logical_not(even))
    def _():
        step(a_new=a1, a_old=a0, w_new=w0, w_old=w1, key_row_in_group=0)


def _peer_dense(u_tab, vt_tab, ht, s1, e1, s2, e2, tau, *, tm, eb, name):
    n_exp, d = u_tab.shape
    t = ht.shape[1]
    assert t % tm == 0 and n_exp % eb == 0 and eb % PEER_N_KEYS == 0
    assert eb // PEER_N_KEYS in (SUBLANES // 2, SUBLANES)
    n_blocks = n_exp // eb
    last = n_blocks - 1
    full = lambda: pl.BlockSpec((PEER_HEADS, PEER_N_KEYS, tm), lambda i, j: (0, 0, i))
    blocks = (2 * _nbytes((eb, d), BF16) * 2 + 2 * _nbytes((d, tm), BF16) + 8 * _nbytes((PEER_HEADS, PEER_N_KEYS, tm), F32)
              + 2 * _nbytes((d, tm), F32) + 2 * _nbytes((eb, tm), F32) + 2 * _nbytes((eb, tm), BF16)
              + 2 * _nbytes((d, tm), F32))
    return pl.pallas_call(
        functools.partial(_peer_dense_kernel, n_blocks=n_blocks, rows_per_block=eb // PEER_N_KEYS,
                          n_lane_tiles=tm // LANES),
        grid=(t // tm, n_blocks + 2),
        in_specs=[
            pl.BlockSpec((eb, d), lambda i, j: (jnp.minimum(j, last), 0)),
            pl.BlockSpec((d, eb), lambda i, j: (0, jnp.clip(j - 2, 0, last))),
            pl.BlockSpec((d, tm), lambda i, j: (0, i)),
            full(), full(), full(), full(),
            pl.BlockSpec((PEER_HEADS, tm), lambda i, j: (0, i)),
        ],
        out_specs=pl.BlockSpec((d, tm), lambda i, j: (0, i)),
        out_shape=jax.ShapeDtypeStruct((d, t), F32),
        scratch_shapes=[pltpu.VMEM((eb, tm), F32), pltpu.VMEM((eb, tm), F32),
                        pltpu.VMEM((eb, tm), BF16), pltpu.VMEM((eb, tm), BF16)],
        compiler_params=_params(("parallel", "arbitrary"), blocks),
        name=name,
    )(u_tab, vt_tab, ht, s1, e1, s2, e2, tau)


def _final_kernel(x_ref, yt_ref, g_ref, o_ref):
    x = x_ref[...] + yt_ref[...].T
    o_ref[...] = (x * lax.rsqrt(jnp.mean(x * x, axis=-1, keepdims=True) + EPS) * g_ref[...]).astype(o_ref.dtype)


def _final(x, yt, g, *, tm, name):
    t, d = x.shape
    blocks = 6 * _nbytes((tm, d), F32) + 2 * _nbytes((tm, d), F32)
    return pl.pallas_call(
        _final_kernel,
        grid=(t // tm,),
        in_specs=[pl.BlockSpec((tm, d), lambda i: (i, 0)), pl.BlockSpec((d, tm), lambda i: (0, i)),
                  pl.BlockSpec((1, d), lambda i: (0, 0))],
        out_specs=pl.BlockSpec((tm, d), lambda i: (i, 0)),
        out_shape=jax.ShapeDtypeStruct((t, d), F32),
        compiler_params=_params(("parallel",), blocks),
        name=name,
    )(x, yt, g.reshape(1, d).astype(F32))


def _t5_bucket(dist):
    max_exact = REL_BUCKETS // 2
    d = np.maximum(dist, 1).astype(np.float64)
    large = max_exact + (np.log(d / max_exact) / math.log(REL_MAX_DIST / max_exact)
                         * (REL_BUCKETS - max_exact)).astype(np.int64)
    large = np.minimum(large, REL_BUCKETS - 1)
    return np.where(dist < max_exact, dist, large).astype(np.int32)


def _prompt_bias_table(rel_bias, group, dil):
    bucket = _t5_bucket(dil * np.arange(STEPS + 1))
    i = np.arange(STEPS)[:, None]
    j = np.arange(2 * STEPS)[None, :]
    off = i + STEPS - j
    band = (off >= 0) & (off <= STEPS)
    bias = rel_bias[bucket, group * KV_HEADS:(group + 1) * KV_HEADS].T.astype(F32)
    return jnp.where(band[None], bias[:, np.clip(off, 0, STEPS)], NEG_INF)


def _sample_bias_table(rel_bias, n_past, n_new, n_keys):
    tabs = []
    cols = np.arange(n_keys)[None, :]
    for g, (win, dil) in enumerate(DIL_PAIRS):
        steps = win // dil
        bucket = _t5_bucket(dil * np.arange(steps + 1))
        bias = rel_bias[bucket, g * KV_HEADS:(g + 1) * KV_HEADS].T.astype(F32)
        pos = n_past + np.arange(n_new)[:, None]
        back = pos - cols
        k = back // dil
        valid = (back >= 0) & (back % dil == 0) & (k <= steps) & (cols < n_past + n_new)
        tabs.append(jnp.where(valid[None], bias[:, np.clip(k, 0, steps)], NEG_INF))
    tabs.append(jnp.zeros((KV_HEADS, n_new, n_keys), F32))
    return jnp.concatenate(tabs, axis=1)


def _input_projection(x, p, *, tm, norm_tm, act_dtype, tag):
    width = SGU_GROUPS * CHUNK
    hn = _rmsnorm(x, p["norm_mix"], tm=norm_tm, out_dtype=BF16, name=f"{tag}_norm_mix")
    mm = functools.partial(_matmul, [hn], tm=tm, tn=width)
    u = mm([p["w_u"]], out_dtype=BF16, epilogue=_epi_gelu, name=f"{tag}_proj_u")
    vs = mm([p["w_vs"]], out_dtype=F32, epilogue=_epi_gelu_layernorm,
            extras=(("row", p["sgu_ln_g"]), ("row", p["sgu_ln_b"])), name=f"{tag}_proj_vs")
    q = mm([p["w_q"]], out_dtype=act_dtype, name=f"{tag}_proj_q")
    kv = mm([p["w_kv"]], out_dtype=F32, name=f"{tag}_proj_kv")
    gates = mm([p["w_gates"]], out_dtype=act_dtype, epilogue=_epi_sigmoid, name=f"{tag}_proj_gates")
    return u, vs, q, kv, gates


def _channel_stages(x, ya, yb, mem_kv, p, *, tm, norm_tm, tl, peer_tm, per_batch, mem_tm, act_dtype, tag):
    d = x.shape[1]
    x1 = _matmul([ya, yb], [p["w_out_a"], p["w_out_b"]], tm=tm, tn=d // 2, out_dtype=F32,
                 epilogue=_epi_residual, extras=(("full", x),), name=f"{tag}_proj_out")
    hq = _rmsnorm(x1, p["norm_mem"], tm=norm_tm, out_dtype=BF16, name=f"{tag}_norm_mem")
    qm = _matmul([hq], [p["w_mq"]], tm=tm, tn=d // 2, out_dtype=act_dtype, name=f"{tag}_mem_q")
    om = _mem_attn(qm, mem_kv, tm=mem_tm, per_batch=per_batch, out_dtype=act_dtype, name=f"{tag}_mem_attn")
    x2 = _matmul([om], [p["w_mo"]], tm=tm, tn=d // 2, out_dtype=F32, epilogue=_epi_residual,
                 extras=(("full", x1),), name=f"{tag}_mem_o")
    ht = _rmsnorm(x2, p["norm_peer"], tm=tl, out_dtype=BF16, transpose=True, name=f"{tag}_norm_peer")
    qt = _matmul([p["peer_wq_t"]], [ht], tm=d // 2, tn=tl, out_dtype=BF16, name=f"{tag}_peer_q")
    s1, e1, s2, e2, tau = _peer_scores(qt, p["peer_keys1"], p["peer_keys2"], tl=tl, name=f"{tag}_peer_scores")
    yt = _peer_dense(p["peer_u"], p["peer_vt"], ht, s1, e1, s2, e2, tau, tm=peer_tm, eb=512,
                     name=f"{tag}_peer_dense")
    return _final(x2, yt, p["norm_final"], tm=tl, name=f"{tag}_final")


def kernel(x_prompt, x_sample, mem_prompt, cache_win, cache_mem_kv, rel_bias, norm_mix, w_in, sgu_ln_g, sgu_ln_b, sgu_w, sgu_b, w_out, norm_mem, norm_memtok, w_mq, w_mk, w_mv, w_mo, norm_peer, peer_wq, peer_keys1, peer_keys2, peer_u, peer_v, norm_final):
    depth = norm_mix.shape[0]
    batch, seq, d = x_prompt.shape
    dec_batch, dec_seq, _ = x_sample.shape
    assert depth == 1 and batch == 1, "one layer and one prompt sequence are supported"
    width = SGU_GROUPS * CHUNK
    n_past = cache_win.shape[2]
    assert seq % (DIL_PAIRS[-1][0]) == 0 and dec_seq <= CHUNK and CHUNK % dec_seq == 0

    w_in_b = w_in[0].astype(BF16)
    o_q = 2 * width
    o_k = o_q + N_DIL * width
    o_g = o_k + 2 * width
    p = dict(
        norm_mix=norm_mix[0], norm_mem=norm_mem[0], norm_peer=norm_peer[0], norm_final=norm_final,
        w_u=w_in_b[:, :width], w_vs=w_in_b[:, width:o_q], w_q=w_in_b[:, o_q:o_k], w_kv=w_in_b[:, o_k:o_g],
        w_gates=w_in_b[:, o_g:],
        sgu_ln_g=sgu_ln_g[0].reshape(1, width), sgu_ln_b=sgu_ln_b[0].reshape(1, width),
        w_out_a=w_out[0, :width].astype(BF16), w_out_b=w_out[0, width:].astype(BF16),
        w_mq=w_mq[0].astype(BF16), w_mo=w_mo[0].astype(BF16),
        peer_wq_t=peer_wq[0].T.astype(BF16),
        peer_keys1=peer_keys1[0].astype(BF16), peer_keys2=peer_keys2[0].astype(BF16),
        peer_u=peer_u[0].astype(BF16), peer_vt=peer_v[0].T.astype(BF16),
    )

    xp = x_prompt.reshape(seq, d)
    u, vs, q, kv, gates = _input_projection(xp, p, tm=PROMPT_TM, norm_tm=NORM_TM, act_dtype=BF16, tag="p")
    tril = np.tril(np.ones((CHUNK, CHUNK), np.float32))
    bias_full = jnp.repeat(sgu_b[0].T, CHUNK, axis=1)
    ya = _sgu(u, vs, gates, sgu_w[0], jnp.asarray(tril), bias_full, n_chunks=4, name="p_sgu")
    outs, lses = [], []
    for g, (win, dil) in enumerate(DIL_PAIRS):
        o_grp, l_grp = _dil_attn(q, kv, _prompt_bias_table(rel_bias, g, dil), group=g, dil=dil, n_sub=4,
                                 name=f"p_dil_attn{g}")
        outs.append(o_grp)
        lses.append(l_grp)
    yb = _combine(outs, lses, gates, tm=NORM_TM, name="p_combine")
    mem_n = _rmsnorm(mem_prompt.reshape(-1, d), norm_memtok[0], tm=mem_prompt.shape[1], out_dtype=BF16,
                     name="p_norm_memtok")
    w_mkv = jnp.concatenate([w_mk[0], w_mv[0]], axis=1).astype(BF16)
    mem_kv = _matmul([mem_n], [w_mkv], tm=mem_n.shape[0], tn=d // 2, out_dtype=F32, name="p_mem_kv")
    y_prompt = _channel_stages(xp, ya, yb, mem_kv, p, tm=PROMPT_TM, norm_tm=NORM_TM, tl=PEER_TOKEN_TILE,
                               peer_tm=PEER_TOKEN_TILE, per_batch=False, mem_tm=PEER_TOKEN_TILE,
                               act_dtype=BF16, tag="p")
    win_rows = min(DIL_PAIRS[-1][0], seq)
    state_win_p = kv[seq - win_rows:].reshape(1, 1, win_rows, 2, KV_HEADS, HEAD_DIM)
    state_mem_p = mem_kv.reshape(1, 1, mem_kv.shape[0], 2, MEM_HEADS, d // MEM_HEADS)

    ts = dec_batch * dec_seq
    assert ts % CHUNK == 0
    xs = x_sample.reshape(ts, d)
    u, vs, q, kv, gates = _input_projection(xs, p, tm=ts, norm_tm=ts, act_dtype=F32, tag="s")
    reps = CHUNK // dec_seq
    blockdiag = np.kron(np.eye(reps, dtype=np.float32), np.tril(np.ones((dec_seq, dec_seq), np.float32)))
    w_small = jnp.tile(sgu_w[0][:, :dec_seq, :dec_seq], (1, reps, reps))
    bias_small = jnp.repeat(jnp.tile(sgu_b[0][:, :dec_seq], (1, reps)).T, CHUNK, axis=1)
    ya = _sgu(u, vs, gates, w_small, jnp.asarray(blockdiag), bias_small, n_chunks=1, name="s_sgu")
    n_keys = -(-(n_past + dec_seq) // LANES) * LANES
    tab = _sample_bias_table(rel_bias, n_past, dec_seq, n_keys)
    cache = cache_win[0].reshape(dec_batch, n_past, 2 * width)
    yb = _smp_attn(q, kv, cache, tab, gates, n_new=dec_seq, out_dtype=F32, name="s_smp_attn")
    mem_s = cache_mem_kv[0].reshape(dec_batch, cache_mem_kv.shape[2], 2 * d)
    y_sample = _channel_stages(xs, ya, yb, mem_s, p, tm=ts, norm_tm=ts, tl=ts, peer_tm=ts, per_batch=True,
                               mem_tm=dec_seq, act_dtype=F32, tag="s")
    state_win_s = kv.reshape(1, dec_batch, dec_seq, 2, KV_HEADS, HEAD_DIM)
    state_sgu_s = vs.reshape(1, dec_batch, dec_seq, width)

    return (y_prompt.reshape(batch, seq, d), y_sample.reshape(dec_batch, dec_seq, d),
            state_win_p, state_mem_p, state_win_s, state_sgu_s)
```

```python
import functools
import math

import numpy as np
import jax
import jax.numpy as jnp
from jax import lax
from jax.experimental import pallas as pl
from jax.experimental.pallas import tpu as pltpu

F32 = jnp.float32
BF16 = jnp.bfloat16

EPS = 1e-6
NEG_INF = -1e30

LANES = 128
SUBLANES = 8
VMEM_PHYSICAL_BYTES = 64 * 1024 * 1024
VMEM_COMPILER_RESERVE_BYTES = 6 * 1024 * 1024

CHUNK = 128
SGU_GROUPS = 8
HEAD_DIM = 128
KV_HEADS = 8
DIL_PAIRS = ((128, 1), (512, 4), (2048, 16))
N_DIL = len(DIL_PAIRS)
STEPS = 128
REL_BUCKETS = 32
REL_MAX_DIST = 2048
MEM_HEADS = 4
PEER_HEADS = 8
PEER_N_KEYS = 128
PEER_KEY_DIM = 128
PEER_TOPK = 16

PROMPT_TM = 1024
NORM_TM = 512
PEER_TOKEN_TILE = 512


def _vmem_limit(block_bytes):
    return int(min(block_bytes + VMEM_COMPILER_RESERVE_BYTES, VMEM_PHYSICAL_BYTES - 4 * 1024 * 1024))


def _nbytes(shape, dtype):
    return int(np.prod(shape)) * jnp.dtype(dtype).itemsize


def _params(semantics, block_bytes):
    return pltpu.CompilerParams(dimension_semantics=semantics, vmem_limit_bytes=_vmem_limit(block_bytes))


def _rmsnorm_kernel(x_ref, g_ref, o_ref, *, transpose):
    x = x_ref[...].astype(F32)
    y = x * lax.rsqrt(jnp.mean(x * x, axis=-1, keepdims=True) + EPS) * g_ref[...]
    if transpose:
        y = y.T
    o_ref[...] = y.astype(o_ref.dtype)


def _rmsnorm(x, g, *, tm, out_dtype, transpose=False, name):
    t, d = x.shape
    assert t % tm == 0
    if transpose:
        out_shape = jax.ShapeDtypeStruct((d, t), out_dtype)
        out_spec = pl.BlockSpec((d, tm), lambda i: (0, i))
    else:
        out_shape = jax.ShapeDtypeStruct((t, d), out_dtype)
        out_spec = pl.BlockSpec((tm, d), lambda i: (i, 0))
    blocks = 2 * _nbytes((tm, d), x.dtype) + 2 * _nbytes((tm, d), out_dtype) + 2 * _nbytes((tm, d), F32)
    return pl.pallas_call(
        functools.partial(_rmsnorm_kernel, transpose=transpose),
        grid=(t // tm,),
        in_specs=[pl.BlockSpec((tm, d), lambda i: (i, 0)), pl.BlockSpec((1, d), lambda i: (0, 0))],
        out_specs=out_spec,
        out_shape=out_shape,
        compiler_params=_params(("parallel",), blocks),
        name=name,
    )(x, g.reshape(1, d).astype(F32))


def _epi_cast(acc):
    return acc


def _epi_gelu(acc):
    return jax.nn.gelu(acc)


def _epi_sigmoid(acc):
    return jax.nn.sigmoid(acc)


def _epi_gelu_layernorm(acc, g, b):
    a = jax.nn.gelu(acc)
    mu = jnp.mean(a, axis=-1, keepdims=True)
    var = jnp.mean(jnp.square(a - mu), axis=-1, keepdims=True)
    return (a - mu) * lax.rsqrt(var + EPS) * g + b


def _epi_residual(acc, res):
    return res + acc


def _matmul_kernel(*refs, n_pairs, n_extras, epilogue):
    out_ref = refs[-1]
    acc = None
    for p in range(n_pairs):
        part = jnp.dot(refs[2 * p][...].astype(BF16), refs[2 * p + 1][...], preferred_element_type=F32)
        acc = part if acc is None else acc + part
    extras = [refs[2 * n_pairs + e][...] for e in range(n_extras)]
    out_ref[...] = epilogue(acc, *extras).astype(out_ref.dtype)


def _matmul(xs, ws, *, tm, tn, out_dtype, epilogue=_epi_cast, extras=(), name):
    m = xs[0].shape[0]
    n = ws[0].shape[1]
    assert m % tm == 0 and n % tn == 0, (m, tm, n, tn)
    in_specs, args, blocks = [], [], 0
    for x, w in zip(xs, ws):
        k = x.shape[1]
        assert w.shape == (k, n) and x.shape == (m, k)
        in_specs += [pl.BlockSpec((tm, k), lambda i, j: (i, 0)), pl.BlockSpec((k, tn), lambda i, j: (0, j))]
        args += [x, w]
        blocks += 2 * _nbytes((tm, k), x.dtype) + 2 * _nbytes((k, tn), w.dtype)
    for kind, arr in extras:
        if kind == "row":
            assert arr.shape == (1, n)
            in_specs.append(pl.BlockSpec((1, tn), lambda i, j: (0, j)))
        else:
            assert kind == "full" and arr.shape == (m, n)
            in_specs.append(pl.BlockSpec((tm, tn), lambda i, j: (i, j)))
            blocks += 2 * _nbytes((tm, tn), arr.dtype)
        args.append(arr)
    blocks += 2 * _nbytes((tm, tn), out_dtype) + 2 * _nbytes((tm, tn), F32)
    return pl.pallas_call(
        functools.partial(_matmul_kernel, n_pairs=len(xs), n_extras=len(extras), epilogue=epilogue),
        grid=(m // tm, n // tn),
        in_specs=in_specs,
        out_specs=pl.BlockSpec((tm, tn), lambda i, j: (i, j)),
        out_shape=jax.ShapeDtypeStruct((m, n), out_dtype),
        compiler_params=_params(("parallel", "parallel"), blocks),
        name=name,
    )(*args)


def _sgu_kernel(u_ref, vs_ref, gate_ref, w_ref, mask_ref, b_ref, o_ref, *, n_chunks):
    mask = mask_ref[...]
    for g in range(SGU_GROUPS):
        cols = slice(g * CHUNK, (g + 1) * CHUNK)
        wm = (w_ref[g] * mask).astype(BF16)
        for c in range(n_chunks):
            rows = slice(c * CHUNK, (c + 1) * CHUNK)
            mixed = jnp.dot(wm, vs_ref[rows, cols].astype(BF16), preferred_element_type=F32) + b_ref[:, cols]
            ya = u_ref[rows, cols].astype(F32) * mixed
            o_ref[rows, cols] = (gate_ref[rows, cols].astype(F32) * ya).astype(o_ref.dtype)


def _sgu(u, vs, gates, w_mix, mask, bias_full, *, n_chunks, name):
    t, width = u.shape
    rows = n_chunks * CHUNK
    assert t % rows == 0 and width == SGU_GROUPS * CHUNK
    blocks = (2 * _nbytes((rows, width), u.dtype) + 2 * _nbytes((rows, width), vs.dtype)
              + 2 * _nbytes((rows, width), gates.dtype) + 2 * _nbytes((rows, width), BF16)
              + 2 * _nbytes(w_mix.shape, F32) + 4 * _nbytes((CHUNK, width), F32))
    return pl.pallas_call(
        functools.partial(_sgu_kernel, n_chunks=n_chunks),
        grid=(t // rows,),
        in_specs=[
            pl.BlockSpec((rows, width), lambda i: (i, 0)),
            pl.BlockSpec((rows, width), lambda i: (i, 0)),
            pl.BlockSpec((rows, width), lambda i: (i, 0)),
            pl.BlockSpec((SGU_GROUPS, CHUNK, CHUNK), lambda i: (0, 0, 0)),
            pl.BlockSpec((CHUNK, CHUNK), lambda i: (0, 0)),
            pl.BlockSpec((CHUNK, width), lambda i: (0, 0)),
        ],
        out_specs=pl.BlockSpec((rows, width), lambda i: (i, 0)),
        out_shape=jax.ShapeDtypeStruct((t, width), BF16),
        compiler_params=_params(("parallel",), blocks),
        name=name,
    )(u, vs, gates, w_mix, mask, bias_full)


def _dil_attn_kernel(q_ref, kc_ref, vc_ref, kp_ref, vp_ref, bias_ref, o_ref, lse_ref, *, n_sub):
    first = pl.program_id(1) == 0
    no_prev = jnp.where(first, NEG_INF, 0.0).astype(F32)
    scale = HEAD_DIM ** -0.5
    nt = (((1,), (1,)), ((), ()))
    for sb in range(n_sub):
        r0 = sb * STEPS
        rows = slice(r0, r0 + STEPS)
        for h in range(KV_HEADS):
            cs = slice(h * HEAD_DIM, (h + 1) * HEAD_DIM)
            q = q_ref[rows, cs]
            if sb == 0:
                ka = kp_ref[:, cs].astype(BF16)
                va = vp_ref[:, cs].astype(BF16)
            else:
                ka = kc_ref[r0 - STEPS:r0, cs].astype(BF16)
                va = vc_ref[r0 - STEPS:r0, cs].astype(BF16)
            kb = kc_ref[rows, cs].astype(BF16)
            vb = vc_ref[rows, cs].astype(BF16)
            la = lax.dot_general(q, ka, nt, preferred_element_type=F32) * scale + bias_ref[h, :, 0:STEPS]
            if sb == 0:
                la = la + no_prev
            lb = lax.dot_general(q, kb, nt, preferred_element_type=F32) * scale + bias_ref[h, :, STEPS:2 * STEPS]
            m = jnp.maximum(jnp.max(la, axis=-1, keepdims=True), jnp.max(lb, axis=-1, keepdims=True))
            ea = jnp.exp(la - m)
            eb = jnp.exp(lb - m)
            s = jnp.sum(ea, axis=-1, keepdims=True) + jnp.sum(eb, axis=-1, keepdims=True)
            o = (jnp.dot(ea.astype(BF16), va, preferred_element_type=F32)
                 + jnp.dot(eb.astype(BF16), vb, preferred_element_type=F32))
            o_ref[rows, cs] = (o / s).astype(o_ref.dtype)
            lse_ref[rows, cs] = jnp.broadcast_to(m + jnp.log(s), (STEPS, HEAD_DIM))


def _dil_attn(q, kv, bias_tab, *, group, dil, n_sub, name):
    s, _ = q.shape
    w = KV_HEADS * HEAD_DIM
    sd = s // dil
    n_sub = math.gcd(n_sub, sd // STEPS)
    tq = n_sub * STEPS
    assert s % (dil * tq) == 0
    q2 = q.reshape(sd, dil * N_DIL * w)
    kv2 = kv.reshape(sd, dil * 2 * w)
    blocks = (2 * _nbytes((tq, w), BF16) + 4 * _nbytes((tq, w), F32) + 4 * _nbytes((STEPS, w), F32)
              + 2 * _nbytes(bias_tab.shape, F32) + 2 * _nbytes((tq, w), BF16) + 2 * _nbytes((tq, w), F32))
    o, lse = pl.pallas_call(
        functools.partial(_dil_attn_kernel, n_sub=n_sub),
        grid=(dil, sd // tq),
        in_specs=[
            pl.BlockSpec((tq, w), lambda r, n: (n, r * N_DIL + group)),
            pl.BlockSpec((tq, w), lambda r, n: (n, r * 2)),
            pl.BlockSpec((tq, w), lambda r, n: (n, r * 2 + 1)),
            pl.BlockSpec((STEPS, w), lambda r, n: (jnp.maximum(n * n_sub - 1, 0), r * 2)),
            pl.BlockSpec((STEPS, w), lambda r, n: (jnp.maximum(n * n_sub - 1, 0), r * 2 + 1)),
            pl.BlockSpec((KV_HEADS, STEPS, 2 * STEPS), lambda r, n: (0, 0, 0)),
        ],
        out_specs=[pl.BlockSpec((tq, w), lambda r, n: (n, r)), pl.BlockSpec((tq, w), lambda r, n: (n, r))],
        out_shape=[jax.ShapeDtypeStruct((sd, dil * w), BF16), jax.ShapeDtypeStruct((sd, dil * w), F32)],
        compiler_params=_params(("parallel", "parallel"), blocks),
        name=name,
    )(q2, kv2, kv2, kv2, kv2, bias_tab)
    return o.reshape(s, w), lse.reshape(s, w)


def _combine_kernel(o0_ref, o1_ref, o2_ref, l0_ref, l1_ref, l2_ref, gate_ref, out_ref):
    l0, l1, l2 = l0_ref[...], l1_ref[...], l2_ref[...]
    mx = jnp.maximum(jnp.maximum(l0, l1), l2)
    w0, w1, w2 = jnp.exp(l0 - mx), jnp.exp(l1 - mx), jnp.exp(l2 - mx)
    mix = w0 * o0_ref[...].astype(F32) + w1 * o1_ref[...].astype(F32) + w2 * o2_ref[...].astype(F32)
    out_ref[...] = (gate_ref[...].astype(F32) * (mix / (w0 + w1 + w2))).astype(out_ref.dtype)


def _combine(outs, lses, gates, *, tm, name):
    t, w = outs[0].shape
    assert t % tm == 0
    blk = lambda: pl.BlockSpec((tm, w), lambda i: (i, 0))
    blocks = 2 * (4 * _nbytes((tm, w), BF16) + 3 * _nbytes((tm, w), F32)) + 8 * _nbytes((tm, w), F32)
    return pl.pallas_call(
        _combine_kernel,
        grid=(t // tm,),
        in_specs=[blk(), blk(), blk(), blk(), blk(), blk(), pl.BlockSpec((tm, w), lambda i: (i, 1))],
        out_specs=blk(),
        out_shape=jax.ShapeDtypeStruct((t, w), BF16),
        compiler_params=_params(("parallel",), blocks),
        name=name,
    )(*outs, *lses, gates)


def _smp_attn_kernel(q0_ref, q1_ref, q2_ref, kn_ref, vn_ref, ck_ref, cv_ref, tab_ref, gate_ref, o_ref,
                     kbuf, vbuf, *, n_past, n_new):
    n_keys = kbuf.shape[0]
    kbuf[0:n_past, :] = ck_ref[...]
    vbuf[0:n_past, :] = cv_ref[...]
    kbuf[n_past:n_past + n_new, :] = kn_ref[...]
    vbuf[n_past:n_past + n_new, :] = vn_ref[...]
    pad = n_keys - n_past - n_new
    kbuf[n_past + n_new:, :] = jnp.zeros((pad, HEAD_DIM), F32)
    vbuf[n_past + n_new:, :] = jnp.zeros((pad, HEAD_DIM), F32)
    q = jnp.concatenate([q0_ref[...].astype(F32), q1_ref[...].astype(F32), q2_ref[...].astype(F32),
                         jnp.zeros((n_new, HEAD_DIM), F32)], axis=0)
    logits = lax.dot_general(q.astype(BF16), kbuf[...].astype(BF16), (((1,), (1,)), ((), ())),
                             preferred_element_type=F32) * (HEAD_DIM ** -0.5) + tab_ref[...]
    m = jnp.max(logits, axis=-1, keepdims=True)
    e = jnp.exp(logits - m)
    s = jnp.sum(e, axis=-1, keepdims=True)
    o = jnp.dot(e.astype(BF16), vbuf[...].astype(BF16), preferred_element_type=F32) / s
    lse = m + jnp.log(s)
    ls = [lse[g * n_new:(g + 1) * n_new] for g in range(N_DIL)]
    mx = jnp.maximum(jnp.maximum(ls[0], ls[1]), ls[2])
    ws = [jnp.exp(l - mx) for l in ls]
    mix = sum(ws[g] * o[g * n_new:(g + 1) * n_new] for g in range(N_DIL)) / (ws[0] + ws[1] + ws[2])
    o_ref[...] = (gate_ref[...].astype(F32) * mix).astype(o_ref.dtype)


def _smp_attn(q, kv, cache, tab, gates, *, n_new, out_dtype, name):
    b, n_past, _ = cache.shape
    n_keys = tab.shape[-1]
    hb = lambda off: pl.BlockSpec((n_new, HEAD_DIM), lambda i, h, off=off: (i, off + h))
    cb = lambda off: pl.BlockSpec((None, n_past, HEAD_DIM), lambda i, h, off=off: (i, 0, off + h))
    blocks = (4 * _nbytes((n_past, HEAD_DIM), F32) + 2 * _nbytes((n_keys, HEAD_DIM), F32)
              + 2 * _nbytes(tab.shape[1:], F32) + 8 * _nbytes((N_DIL * n_new, n_keys), F32))
    return pl.pallas_call(
        functools.partial(_smp_attn_kernel, n_past=n_past, n_new=n_new),
        grid=(b, KV_HEADS),
        in_specs=[
            hb(0), hb(KV_HEADS), hb(2 * KV_HEADS),
            hb(0), hb(KV_HEADS),
            cb(0), cb(KV_HEADS),
            pl.BlockSpec((None, (N_DIL + 1) * n_new, n_keys), lambda i, h: (h, 0, 0)),
            hb(KV_HEADS),
        ],
        out_specs=pl.BlockSpec((n_new, HEAD_DIM), lambda i, h: (i, h)),
        out_shape=jax.ShapeDtypeStruct((b * n_new, KV_HEADS * HEAD_DIM), out_dtype),
        scratch_shapes=[pltpu.VMEM((n_keys, HEAD_DIM), F32), pltpu.VMEM((n_keys, HEAD_DIM), F32)],
        compiler_params=_params(("parallel", "parallel"), blocks),
        name=name,
    )(q, q, q, kv, kv, cache, cache, tab, gates)


def _mem_attn_kernel(q_ref, k_ref, v_ref, o_ref, *, head_dim):
    scale = head_dim ** -0.5
    for h in range(MEM_HEADS):
        cs = slice(h * head_dim, (h + 1) * head_dim)
        logits = lax.dot_general(q_ref[:, cs].astype(BF16), k_ref[:, cs].astype(BF16), (((1,), (1,)), ((), ())),
                                 preferred_element_type=F32) * scale
        m = jnp.max(logits, axis=-1, keepdims=True)
        e = jnp.exp(logits - m)
        s = jnp.sum(e, axis=-1, keepdims=True)
        o = jnp.dot(e.astype(BF16), v_ref[:, cs].astype(BF16), preferred_element_type=F32)
        o_ref[:, cs] = (o / s).astype(o_ref.dtype)


def _mem_attn(q, mem_kv, *, tm, per_batch, out_dtype, name):
    t, w = q.shape
    m_tok = mem_kv.shape[-2]
    if per_batch:
        kspec = pl.BlockSpec((None, m_tok, w), lambda i: (i, 0, 0))
        vspec = pl.BlockSpec((None, m_tok, w), lambda i: (i, 0, 1))
    else:
        kspec = pl.BlockSpec((m_tok, w), lambda i: (0, 0))
        vspec = pl.BlockSpec((m_tok, w), lambda i: (0, 1))
    blocks = 4 * _nbytes((tm, w), BF16) + 4 * _nbytes((m_tok, w), F32) + 6 * _nbytes((tm, m_tok), F32) \
        + 2 * _nbytes((tm, w), F32)
    return pl.pallas_call(
        functools.partial(_mem_attn_kernel, head_dim=w // MEM_HEADS),
        grid=(t // tm,),
        in_specs=[pl.BlockSpec((tm, w), lambda i: (i, 0)), kspec, vspec],
        out_specs=pl.BlockSpec((tm, w), lambda i: (i, 0)),
        out_shape=jax.ShapeDtypeStruct((t, w), out_dtype),
        compiler_params=_params(("parallel",), blocks),
        name=name,
    )(q, mem_kv, mem_kv)


def _oddeven_merge_sort_pairs(n):
    pairs = []

    def merge(lo, hi, r):
        step = r * 2
        if step < hi - lo:
            merge(lo, hi, step)
            merge(lo + r, hi, step)
            pairs.extend((i, i + r) for i in range(lo + r, hi - r, step))
        else:
            pairs.append((lo, lo + r))

    def sort(lo, hi):
        if hi - lo >= 1:
            mid = lo + (hi - lo) // 2
            sort(lo, mid)
            sort(mid + 1, hi)
            merge(lo, hi, 1)

    sort(0, n - 1)
    return pairs


def _compare_exchange(wires, i, j):
    a, b = wires[i], wires[j]
    if b is None:
        return
    if a is None:
        wires[i], wires[j] = b, None
        return
    wires[i], wires[j] = jnp.maximum(a, b), jnp.minimum(a, b)


def _sort_desc(wires):
    for i, j in _oddeven_merge_sort_pairs(len(wires)):
        _compare_exchange(wires, i, j)
    return wires


def _bitonic_merge_desc(wires):
    n = len(wires)
    dist = n // 2
    while dist >= 1:
        for i in range(n):
            if i & dist == 0:
                _compare_exchange(wires, i, i + dist)
        dist //= 2
    return wires


def _top16_desc(s):
    wires = [s[SUBLANES * i:SUBLANES * (i + 1), :] for i in range(PEER_N_KEYS // SUBLANES)]
    wires = _sort_desc(wires)
    for shift in (4, 2, 1):
        wires = [jnp.maximum(wires[i], pltpu.roll(wires[PEER_TOPK - 1 - i], shift, 0)) for i in range(PEER_TOPK)]
        wires = _bitonic_merge_desc(wires)
    return wires


_CAND_PAIRS = [(j, k) for j in range(PEER_TOPK) for k in range(PEER_TOPK) if (j + 1) * (k + 1) <= PEER_TOPK]


def _peer_scores_kernel(qt_ref, k1_ref, k2_ref, s1_ref, e1_ref, s2_ref, e2_ref, tau_ref, *, n_lane_tiles):
    for h in range(PEER_HEADS):
        r0 = h * 2 * PEER_KEY_DIM
        s1_ref[h] = jnp.dot(k1_ref[h], qt_ref[r0:r0 + PEER_KEY_DIM, :], preferred_element_type=F32)
        s2_ref[h] = jnp.dot(k2_ref[h], qt_ref[r0 + PEER_KEY_DIM:r0 + 2 * PEER_KEY_DIM, :],
                            preferred_element_type=F32)
    sub = lax.broadcasted_iota(jnp.int32, (SUBLANES, LANES), 0)

    def lane_tile(lt, carry):
        lanes = pl.ds(pl.multiple_of(lt * LANES, LANES), LANES)
        a_pack = [None] * PEER_TOPK
        b_pack = [None] * PEER_TOPK
        for h in range(PEER_HEADS):
            ta = _top16_desc(s1_ref[h, :, lanes])
            tb = _top16_desc(s2_ref[h, :, lanes])
            for j in range(PEER_TOPK):
                a_pack[j] = ta[j] if h == 0 else jnp.where(sub == h, ta[j], a_pack[j])
                b_pack[j] = tb[j] if h == 0 else jnp.where(sub == h, tb[j], b_pack[j])
        cands = [a_pack[j] + b_pack[k] for j, k in _CAND_PAIRS]
        top = cands[0]
        wires = _sort_desc(list(cands) + [None] * (64 - len(cands)))
        tau = wires[PEER_TOPK - 1]
        z = jnp.zeros_like(tau)
        for c in cands:
            z = z + jnp.where(c >= tau, jnp.exp(c - top), 0.0)
        zinv = 1.0 / z
        tau_ref[:, lanes] = tau
        for h in range(PEER_HEADS):
            e1_ref[h, :, lanes] = jnp.exp(s1_ref[h, :, lanes] - a_pack[0][h:h + 1, :])
            e2_ref[h, :, lanes] = jnp.exp(s2_ref[h, :, lanes] - b_pack[0][h:h + 1, :]) * zinv[h:h + 1, :]
        return carry

    lax.fori_loop(0, n_lane_tiles, lane_tile, 0)


def _peer_scores(qt, keys1, keys2, *, tl, name):
    d, t = qt.shape
    assert t % tl == 0 and d == PEER_HEADS * 2 * PEER_KEY_DIM
    big = jax.ShapeDtypeStruct((PEER_HEADS, PEER_N_KEYS, t), F32)
    bspec = lambda: pl.BlockSpec((PEER_HEADS, PEER_N_KEYS, tl), lambda i: (0, 0, i))
    blocks = 2 * _nbytes((d, tl), BF16) + 4 * _nbytes(keys1.shape, BF16) + 8 * _nbytes((PEER_HEADS, PEER_N_KEYS, tl), F32)
    return pl.pallas_call(
        functools.partial(_peer_scores_kernel, n_lane_tiles=tl // LANES),
        grid=(t // tl,),
        in_specs=[
            pl.BlockSpec((d, tl), lambda i: (0, i)),
            pl.BlockSpec(keys1.shape, lambda i: (0, 0, 0)),
            pl.BlockSpec(keys2.shape, lambda i: (0, 0, 0)),
        ],
        out_specs=[bspec(), bspec(), bspec(), bspec(), pl.BlockSpec((PEER_HEADS, tl), lambda i: (0, i))],
        out_shape=[big, big, big, big, jax.ShapeDtypeStruct((PEER_HEADS, t), F32)],
        compiler_params=_params(("parallel",), blocks),
        name=name,
    )(qt, keys1, keys2)


def _peer_dense_step(u_ref, vt_ref, ht_ref, s1_ref, e1_ref, s2_ref, e2_ref, tau_ref, o_ref,
                     a_new, a_old, w_new, w_old, key_row_in_group, rows_per_block, n_lane_tiles):
    half = PEER_N_KEYS // 2

    def gate_lane_tile(lt):
        lanes = slice(lt * LANES, (lt + 1) * LANES)
        for i0 in range(0, rows_per_block, 2):
            for hf in range(2):
                krows = slice(hf * half, (hf + 1) * half)
                gates = {}
                for h in range(PEER_HEADS):
                    s1_rows = s1_ref[h, :, lanes]
                    e1_rows = e1_ref[h, :, lanes]
                    tau = tau_ref[h:h + 1, lanes]
                    s2t = s2_ref[h, krows, lanes]
                    e2t = e2_ref[h, krows, lanes]
                    for ii in (i0, i0 + 1):
                        r = key_row_in_group + ii
                        hit = (s1_rows[r:r + 1, :] + s2t) >= tau
                        term = jnp.where(hit, e1_rows[r:r + 1, :] * e2t, 0.0)
                        gates[ii] = term if h == 0 else gates[ii] + term
                for ii, gate in gates.items():
                    erows = slice(ii * PEER_N_KEYS + hf * half, ii * PEER_N_KEYS + (hf + 1) * half)
                    w_new[erows, lanes] = (gate * jax.nn.gelu(a_old[erows, lanes])).astype(w_new.dtype)

    for c in range(n_lane_tiles // 2):
        lanes2 = slice(2 * c * LANES, 2 * (c + 1) * LANES)
        a_new[:, lanes2] = jnp.dot(u_ref[...], ht_ref[:, lanes2], preferred_element_type=F32)
        gate_lane_tile(2 * c)
        o_ref[:, lanes2] += jnp.dot(vt_ref[...], w_old[:, lanes2], preferred_element_type=F32)
        gate_lane_tile(2 * c + 1)


def _peer_dense_kernel(u_ref, vt_ref, ht_ref, s1_ref, e1_ref, s2_ref, e2_ref, tau_ref, o_ref,
                       a0, a1, w0, w1, *, rows_per_block, n_lane_tiles):
    j = pl.program_id(1)

    @pl.when(j == 0)
    def _():
        o_ref[...] = jnp.zeros_like(o_ref)
        a1[...] = jnp.zeros_like(a1)
        w0[...] = jnp.zeros_like(w0)

    blocks_per_group = SUBLANES // rows_per_block
    step = functools.partial(_peer_dense_step, u_ref, vt_ref, ht_ref, s1_ref, e1_ref, s2_ref, e2_ref, tau_ref,
                             o_ref, rows_per_block=rows_per_block, n_lane_tiles=n_lane_tiles)
    even = lax.rem(j, 2) == 0

    @pl.when(even)
    def _():
        step(a_new=a0, a_old=a1, w_new=w1, w_old=w0, key_row_in_group=(blocks_per_group - 1) * rows_per_block)

    @pl.when(jnp.logical_not(even))
    def _():
        step(a_new=a1, a_old=a0, w_new=w0, w_old=w1, key_row_in_group=0)


def _peer_dense(u_tab, vt_tab, ht, s1, e1, s2, e2, tau, *, tm, eb, name):
    n_exp, d = u_tab.shape
    t = ht.shape[1]
    assert t % tm == 0 and tm % (2 * LANES) == 0 and n_exp % eb == 0 and eb % PEER_N_KEYS == 0
    assert eb // PEER_N_KEYS in (SUBLANES // 2, SUBLANES)
    n_blocks = n_exp // eb
    last = n_blocks - 1
    full = lambda: pl.BlockSpec((PEER_HEADS, PEER_N_KEYS, tm), lambda i, j: (0, 0, i))
    blocks_per_group = SUBLANES * PEER_N_KEYS // eb
    group = lambda: pl.BlockSpec((PEER_HEADS, SUBLANES, tm),
                                 lambda i, j: (0, jnp.clip(j - 1, 0, last) // blocks_per_group, i))
    blocks = (2 * _nbytes((eb, d), BF16) * 2 + 2 * _nbytes((d, tm), BF16) + 8 * _nbytes((PEER_HEADS, PEER_N_KEYS, tm), F32)
              + 2 * _nbytes((d, tm), F32) + 2 * _nbytes((eb, tm), F32) + 2 * _nbytes((eb, tm), BF16)
              + 2 * _nbytes((d, tm), F32))
    return pl.pallas_call(
        functools.partial(_peer_dense_kernel, rows_per_block=eb // PEER_N_KEYS, n_lane_tiles=tm // LANES),
        grid=(t // tm, n_blocks + 2),
        in_specs=[
            pl.BlockSpec((eb, d), lambda i, j: (jnp.minimum(j, last), 0)),
            pl.BlockSpec((d, eb), lambda i, j: (0, jnp.clip(j - 2, 0, last))),
            pl.BlockSpec((d, tm), lambda i, j: (0, i)),
            group(), group(), full(), full(),
            pl.BlockSpec((PEER_HEADS, tm), lambda i, j: (0, i)),
        ],
        out_specs=pl.BlockSpec((d, tm), lambda i, j: (0, i)),
        out_shape=jax.ShapeDtypeStruct((d, t), F32),
        scratch_shapes=[pltpu.VMEM((eb, tm), F32), pltpu.VMEM((eb, tm), F32),
                        pltpu.VMEM((eb, tm), BF16), pltpu.VMEM((eb, tm), BF16)],
        compiler_params=_params(("parallel", "arbitrary"), blocks),
        name=name,
    )(u_tab, vt_tab, ht, s1, e1, s2, e2, tau)


def _final_kernel(x_ref, yt_ref, g_ref, o_ref):
    x = x_ref[...] + yt_ref[...].T
    o_ref[...] = (x * lax.rsqrt(jnp.mean(x * x, axis=-1, keepdims=True) + EPS) * g_ref[...]).astype(o_ref.dtype)


def _final(x, yt, g, *, tm, name):
    t, d = x.shape
    blocks = 6 * _nbytes((tm, d), F32) + 2 * _nbytes((tm, d), F32)
    return pl.pallas_call(
        _final_kernel,
        grid=(t // tm,),
        in_specs=[pl.BlockSpec((tm, d), lambda i: (i, 0)), pl.BlockSpec((d, tm), lambda i: (0, i)),
                  pl.BlockSpec((1, d), lambda i: (0, 0))],
        out_specs=pl.BlockSpec((tm, d), lambda i: (i, 0)),
        out_shape=jax.ShapeDtypeStruct((t, d), F32),
        compiler_params=_params(("parallel",), blocks),
        name=name,
    )(x, yt, g.reshape(1, d).astype(F32))


def _t5_bucket(dist):
    max_exact = REL_BUCKETS // 2
    d = np.maximum(dist, 1).astype(np.float64)
    large = max_exact + (np.log(d / max_exact) / math.log(REL_MAX_DIST / max_exact)
                         * (REL_BUCKETS - max_exact)).astype(np.int64)
    large = np.minimum(large, REL_BUCKETS - 1)
    return np.where(dist < max_exact, dist, large).astype(np.int32)


def _spread_bias(b):
    return jnp.broadcast_to(b.T[:, :, None], (CHUNK, SGU_GROUPS, CHUNK)).reshape(CHUNK, SGU_GROUPS * CHUNK)


def _group_bias(rel_bias, group, dil):
    bucket = _t5_bucket(dil * np.arange(STEPS + 1))
    table = rel_bias[:, group * KV_HEADS:(group + 1) * KV_HEADS].astype(F32)
    runs, start = [], 0
    for i in range(1, len(bucket) + 1):
        if i == len(bucket) or bucket[i] != bucket[start]:
            runs.append(jnp.broadcast_to(table[bucket[start]:bucket[start] + 1], (i - start, KV_HEADS)))
            start = i
    return jnp.concatenate(runs, axis=0).T


def _prompt_bias_table(rel_bias, group, dil):
    period = 3 * STEPS
    bias = _group_bias(rel_bias, group, dil)
    vec = jnp.concatenate([bias[:, ::-1], jnp.full((KV_HEADS, period - STEPS - 1), NEG_INF, F32)], axis=1)
    flat = jnp.broadcast_to(vec[:, None, :], (KV_HEADS, STEPS, period)).reshape(KV_HEADS, STEPS * period)
    skew = flat[:, :STEPS * (period - 1)].reshape(KV_HEADS, STEPS, period - 1)
    return skew[:, :, :2 * STEPS]


def _sample_bias_table(rel_bias, n_past, n_new, n_keys):
    rows = []
    for g, (win, dil) in enumerate(DIL_PAIRS):
        rev = _group_bias(rel_bias, g, dil)[:, ::-1]
        for t in range(n_new):
            lo = n_past + t - win
            hi = n_keys - (n_past + t + 1)
            rows.append(lax.pad(rev, jnp.asarray(NEG_INF, F32), [(0, 0, 0), (lo, hi, dil - 1)]))
    rows += [jnp.zeros((KV_HEADS, n_keys), F32)] * n_new
    return jnp.stack(rows, axis=1)


def _input_projection(x, p, *, tm, norm_tm, act_dtype, tag):
    width = SGU_GROUPS * CHUNK
    hn = _rmsnorm(x, p["norm_mix"], tm=norm_tm, out_dtype=BF16, name=f"{tag}_norm_mix")
    mm = functools.partial(_matmul, [hn], tm=tm, tn=width)
    u = mm([p["w_u"]], out_dtype=BF16, epilogue=_epi_gelu, name=f"{tag}_proj_u")
    vs = mm([p["w_vs"]], out_dtype=F32, epilogue=_epi_gelu_layernorm,
            extras=(("row", p["sgu_ln_g"]), ("row", p["sgu_ln_b"])), name=f"{tag}_proj_vs")
    q = mm([p["w_q"]], out_dtype=act_dtype, name=f"{tag}_proj_q")
    kv = mm([p["w_kv"]], out_dtype=F32, name=f"{tag}_proj_kv")
    gates = mm([p["w_gates"]], out_dtype=act_dtype, epilogue=_epi_sigmoid, name=f"{tag}_proj_gates")
    return u, vs, q, kv, gates


def _channel_stages(x, ya, yb, mem_kv, p, *, tm, norm_tm, tl, peer_tm, per_batch, mem_tm, act_dtype, tag):
    d = x.shape[1]
    x1 = _matmul([ya, yb], [p["w_out_a"], p["w_out_b"]], tm=tm, tn=d // 2, out_dtype=F32,
                 epilogue=_epi_residual, extras=(("full", x),), name=f"{tag}_proj_out")
    hq = _rmsnorm(x1, p["norm_mem"], tm=norm_tm, out_dtype=BF16, name=f"{tag}_norm_mem")
    qm = _matmul([hq], [p["w_mq"]], tm=tm, tn=d // 2, out_dtype=act_dtype, name=f"{tag}_mem_q")
    om = _mem_attn(qm, mem_kv, tm=mem_tm, per_batch=per_batch, out_dtype=act_dtype, name=f"{tag}_mem_attn")
    x2 = _matmul([om], [p["w_mo"]], tm=tm, tn=d // 2, out_dtype=F32, epilogue=_epi_residual,
                 extras=(("full", x1),), name=f"{tag}_mem_o")
    ht = _rmsnorm(x2, p["norm_peer"], tm=tl, out_dtype=BF16, transpose=True, name=f"{tag}_norm_peer")
    qt = _matmul([p["peer_wq_t"]], [ht], tm=d // 2, tn=tl, out_dtype=BF16, name=f"{tag}_peer_q")
    s1, e1, s2, e2, tau = _peer_scores(qt, p["peer_keys1"], p["peer_keys2"], tl=tl, name=f"{tag}_peer_scores")
    yt = _peer_dense(p["peer_u"], p["peer_vt"], ht, s1, e1, s2, e2, tau, tm=peer_tm, eb=512,
                     name=f"{tag}_peer_dense")
    return _final(x2, yt, p["norm_final"], tm=tl, name=f"{tag}_final")


def kernel(x_prompt, x_sample, mem_prompt, cache_win, cache_mem_kv, rel_bias, norm_mix, w_in, sgu_ln_g, sgu_ln_b, sgu_w, sgu_b, w_out, norm_mem, norm_memtok, w_mq, w_mk, w_mv, w_mo, norm_peer, peer_wq, peer_keys1, peer_keys2, peer_u, peer_v, norm_final):
    depth = norm_mix.shape[0]
    batch, seq, d = x_prompt.shape
    dec_batch, dec_seq, _ = x_sample.shape
    assert depth == 1 and batch == 1, "one layer and one prompt sequence are supported"
    width = SGU_GROUPS * CHUNK
    n_past = cache_win.shape[2]
    assert seq % (DIL_PAIRS[-1][0]) == 0 and dec_seq <= CHUNK and CHUNK % dec_seq == 0

    w_in_b = w_in[0].astype(BF16)
    o_q = 2 * width
    o_k = o_q + N_DIL * width
    o_g = o_k + 2 * width
    p = dict(
        norm_mix=norm_mix[0], norm_mem=norm_mem[0], norm_peer=norm_peer[0], norm_final=norm_final,
        w_u=w_in_b[:, :width], w_vs=w_in_b[:, width:o_q], w_q=w_in_b[:, o_q:o_k], w_kv=w_in_b[:, o_k:o_g],
        w_gates=w_in_b[:, o_g:],
        sgu_ln_g=sgu_ln_g[0].reshape(1, width), sgu_ln_b=sgu_ln_b[0].reshape(1, width),
        w_out_a=w_out[0, :width].astype(BF16), w_out_b=w_out[0, width:].astype(BF16),
        w_mq=w_mq[0].astype(BF16), w_mo=w_mo[0].astype(BF16),
        peer_wq_t=peer_wq[0].T.astype(BF16),
        peer_keys1=peer_keys1[0].astype(BF16), peer_keys2=peer_keys2[0].astype(BF16),
        peer_u=peer_u[0].astype(BF16), peer_vt=peer_v[0].T.astype(BF16),
    )

    xp = x_prompt.reshape(seq, d)
    u, vs, q, kv, gates = _input_projection(xp, p, tm=PROMPT_TM, norm_tm=NORM_TM, act_dtype=BF16, tag="p")
    tril = np.tril(np.ones((CHUNK, CHUNK), np.float32))
    bias_full = _spread_bias(sgu_b[0])
    ya = _sgu(u, vs, gates, sgu_w[0], jnp.asarray(tril), bias_full, n_chunks=4, name="p_sgu")
    outs, lses = [], []
    for g, (win, dil) in enumerate(DIL_PAIRS):
        o_grp, l_grp = _dil_attn(q, kv, _prompt_bias_table(rel_bias, g, dil), group=g, dil=dil, n_sub=4,
                                 name=f"p_dil_attn{g}")
        outs.append(o_grp)
        lses.append(l_grp)
    yb = _combine(outs, lses, gates, tm=NORM_TM, name="p_combine")
    mem_n = _rmsnorm(mem_prompt.reshape(-1, d), norm_memtok[0], tm=mem_prompt.shape[1], out_dtype=BF16,
                     name="p_norm_memtok")
    w_mkv = jnp.concatenate([w_mk[0], w_mv[0]], axis=1).astype(BF16)
    mem_kv = _matmul([mem_n], [w_mkv], tm=mem_n.shape[0], tn=d // 2, out_dtype=F32, name="p_mem_kv")
    y_prompt = _channel_stages(xp, ya, yb, mem_kv, p, tm=PROMPT_TM, norm_tm=NORM_TM, tl=PEER_TOKEN_TILE,
                               peer_tm=PEER_TOKEN_TILE, per_batch=False, mem_tm=PEER_TOKEN_TILE,
                               act_dtype=BF16, tag="p")
    win_rows = min(DIL_PAIRS[-1][0], seq)
    state_win_p = kv[seq - win_rows:].reshape(1, 1, win_rows, 2, KV_HEADS, HEAD_DIM)
    state_mem_p = mem_kv.reshape(1, 1, mem_kv.shape[0], 2, MEM_HEADS, d // MEM_HEADS)

    ts = dec_batch * dec_seq
    assert ts % CHUNK == 0
    xs = x_sample.reshape(ts, d)
    u, vs, q, kv, gates = _input_projection(xs, p, tm=ts, norm_tm=ts, act_dtype=F32, tag="s")
    reps = CHUNK // dec_seq
    blockdiag = np.kron(np.eye(reps, dtype=np.float32), np.tril(np.ones((dec_seq, dec_seq), np.float32)))
    w_small = jnp.tile(sgu_w[0][:, :dec_seq, :dec_seq], (1, reps, reps))
    bias_small = _spread_bias(jnp.tile(sgu_b[0][:, :dec_seq], (1, reps)))
    ya = _sgu(u, vs, gates, w_small, jnp.asarray(blockdiag), bias_small, n_chunks=1, name="s_sgu")
    n_keys = -(-(n_past + dec_seq) // LANES) * LANES
    tab = _sample_bias_table(rel_bias, n_past, dec_seq, n_keys)
    cache = cache_win[0].reshape(dec_batch, n_past, 2 * width)
    yb = _smp_attn(q, kv, cache, tab, gates, n_new=dec_seq, out_dtype=F32, name="s_smp_attn")
    mem_s = cache_mem_kv[0].reshape(dec_batch, cache_mem_kv.shape[2], 2 * d)
    y_sample = _channel_stages(xs, ya, yb, mem_s, p, tm=ts, norm_tm=ts, tl=ts, peer_tm=ts, per_batch=True,
                               mem_tm=dec_seq, act_dtype=F32, tag="s")
    state_win_s = kv.reshape(1, dec_batch, dec_seq, 2, KV_HEADS, HEAD_DIM)
    state_sgu_s = vs.reshape(1, dec_batch, dec_seq, width)

    return (y_prompt.reshape(batch, seq, d), y_sample.reshape(dec_batch, dec_seq, d),
            state_win_p, state_mem_p, state_win_s, state_sgu_s)
```

```python
import functools
import math

import numpy as np
import jax
import jax.numpy as jnp
from jax import lax
from jax.experimental import pallas as pl
from jax.experimental.pallas import tpu as pltpu

F32 = jnp.float32
BF16 = jnp.bfloat16

EPS = 1e-6
NEG_INF = -1e30

LANES = 128
SUBLANES = 8
VMEM_PHYSICAL_BYTES = 64 * 1024 * 1024
VMEM_COMPILER_RESERVE_BYTES = 6 * 1024 * 1024

CHUNK = 128
SGU_GROUPS = 8
HEAD_DIM = 128
KV_HEADS = 8
DIL_PAIRS = ((128, 1), (512, 4), (2048, 16))
N_DIL = len(DIL_PAIRS)
STEPS = 128
REL_BUCKETS = 32
REL_MAX_DIST = 2048
MEM_HEADS = 4
PEER_HEADS = 8
PEER_N_KEYS = 128
PEER_KEY_DIM = 128
PEER_TOPK = 16

PROMPT_TM = 1024
NORM_TM = 512
PEER_TOKEN_TILE = 512


def _vmem_limit(block_bytes):
    return int(min(block_bytes + VMEM_COMPILER_RESERVE_BYTES, VMEM_PHYSICAL_BYTES - 4 * 1024 * 1024))


def _nbytes(shape, dtype):
    return int(np.prod(shape)) * jnp.dtype(dtype).itemsize


def _params(semantics, block_bytes):
    return pltpu.CompilerParams(dimension_semantics=semantics, vmem_limit_bytes=_vmem_limit(block_bytes))


def _rmsnorm_kernel(x_ref, g_ref, o_ref, *, transpose):
    x = x_ref[...].astype(F32)
    y = x * lax.rsqrt(jnp.mean(x * x, axis=-1, keepdims=True) + EPS) * g_ref[...]
    if transpose:
        y = y.T
    o_ref[...] = y.astype(o_ref.dtype)


def _rmsnorm(x, g, *, tm, out_dtype, transpose=False, name):
    t, d = x.shape
    assert t % tm == 0
    if transpose:
        out_shape = jax.ShapeDtypeStruct((d, t), out_dtype)
        out_spec = pl.BlockSpec((d, tm), lambda i: (0, i))
    else:
        out_shape = jax.ShapeDtypeStruct((t, d), out_dtype)
        out_spec = pl.BlockSpec((tm, d), lambda i: (i, 0))
    blocks = 2 * _nbytes((tm, d), x.dtype) + 2 * _nbytes((tm, d), out_dtype) + 2 * _nbytes((tm, d), F32)
    return pl.pallas_call(
        functools.partial(_rmsnorm_kernel, transpose=transpose),
        grid=(t // tm,),
        in_specs=[pl.BlockSpec((tm, d), lambda i: (i, 0)), pl.BlockSpec((1, d), lambda i: (0, 0))],
        out_specs=out_spec,
        out_shape=out_shape,
        compiler_params=_params(("parallel",), blocks),
        name=name,
    )(x, g.reshape(1, d).astype(F32))


def _epi_cast(acc):
    return acc


def _epi_gelu(acc):
    return jax.nn.gelu(acc)


def _epi_sigmoid(acc):
    return jax.nn.sigmoid(acc)


def _epi_gelu_layernorm(acc, g, b):
    a = jax.nn.gelu(acc)
    mu = jnp.mean(a, axis=-1, keepdims=True)
    var = jnp.mean(jnp.square(a - mu), axis=-1, keepdims=True)
    return (a - mu) * lax.rsqrt(var + EPS) * g + b


def _epi_residual(acc, res):
    return res + acc


def _matmul_kernel(*refs, n_pairs, n_extras, epilogue, head_major):
    out_ref = refs[-1]
    acc = None
    for p in range(n_pairs):
        part = jnp.dot(refs[2 * p][...].astype(BF16), refs[2 * p + 1][...], preferred_element_type=F32)
        acc = part if acc is None else acc + part
    extras = [refs[2 * n_pairs + e][...] for e in range(n_extras)]
    res = epilogue(acc, *extras).astype(out_ref.dtype)
    if head_major:
        for hh in range(out_ref.shape[0]):
            out_ref[hh] = res[:, hh * HEAD_DIM:(hh + 1) * HEAD_DIM]
    else:
        out_ref[...] = res


def _matmul(xs, ws, *, tm, tn, out_dtype, epilogue=_epi_cast, extras=(), head_major=False, name):
    m = xs[0].shape[0]
    n = ws[0].shape[1]
    assert m % tm == 0 and n % tn == 0, (m, tm, n, tn)
    in_specs, args, blocks = [], [], 0
    for x, w in zip(xs, ws):
        k = x.shape[1]
        assert w.shape == (k, n) and x.shape == (m, k)
        in_specs += [pl.BlockSpec((tm, k), lambda i, j: (i, 0)), pl.BlockSpec((k, tn), lambda i, j: (0, j))]
        args += [x, w]
        blocks += 2 * _nbytes((tm, k), x.dtype) + 2 * _nbytes((k, tn), w.dtype)
    for kind, arr in extras:
        if kind == "row":
            assert arr.shape == (1, n)
            in_specs.append(pl.BlockSpec((1, tn), lambda i, j: (0, j)))
        else:
            assert kind == "full" and arr.shape == (m, n)
            in_specs.append(pl.BlockSpec((tm, tn), lambda i, j: (i, j)))
            blocks += 2 * _nbytes((tm, tn), arr.dtype)
        args.append(arr)
    blocks += 2 * _nbytes((tm, tn), out_dtype) + 2 * _nbytes((tm, tn), F32)
    if head_major:
        out_spec = pl.BlockSpec((tn // HEAD_DIM, tm, HEAD_DIM), lambda i, j: (j, i, 0))
        out_shape = jax.ShapeDtypeStruct((n // HEAD_DIM, m, HEAD_DIM), out_dtype)
    else:
        out_spec = pl.BlockSpec((tm, tn), lambda i, j: (i, j))
        out_shape = jax.ShapeDtypeStruct((m, n), out_dtype)
    return pl.pallas_call(
        functools.partial(_matmul_kernel, n_pairs=len(xs), n_extras=len(extras), epilogue=epilogue,
                          head_major=head_major),
        grid=(m // tm, n // tn),
        in_specs=in_specs,
        out_specs=out_spec,
        out_shape=out_shape,
        compiler_params=_params(("parallel", "parallel"), blocks),
        name=name,
    )(*args)


def _sgu_kernel(u_ref, vs_ref, gate_ref, w_ref, mask_ref, b_ref, o_ref, *, n_chunks):
    mask = mask_ref[...]
    for g in range(SGU_GROUPS):
        cols = slice(g * CHUNK, (g + 1) * CHUNK)
        wm = (w_ref[g] * mask).astype(BF16)
        for c in range(n_chunks):
            rows = slice(c * CHUNK, (c + 1) * CHUNK)
            mixed = jnp.dot(wm, vs_ref[rows, cols].astype(BF16), preferred_element_type=F32) + b_ref[:, cols]
            ya = u_ref[rows, cols].astype(F32) * mixed
            o_ref[rows, cols] = (gate_ref[rows, cols].astype(F32) * ya).astype(o_ref.dtype)


def _sgu(u, vs, gates, w_mix, mask, bias_full, *, n_chunks, name):
    t, width = u.shape
    rows = n_chunks * CHUNK
    assert t % rows == 0 and width == SGU_GROUPS * CHUNK
    blocks = (2 * _nbytes((rows, width), u.dtype) + 2 * _nbytes((rows, width), vs.dtype)
              + 2 * _nbytes((rows, width), gates.dtype) + 2 * _nbytes((rows, width), BF16)
              + 2 * _nbytes(w_mix.shape, F32) + 4 * _nbytes((CHUNK, width), F32))
    return pl.pallas_call(
        functools.partial(_sgu_kernel, n_chunks=n_chunks),
        grid=(t // rows,),
        in_specs=[
            pl.BlockSpec((rows, width), lambda i: (i, 0)),
            pl.BlockSpec((rows, width), lambda i: (i, 0)),
            pl.BlockSpec((rows, width), lambda i: (i, 0)),
            pl.BlockSpec((SGU_GROUPS, CHUNK, CHUNK), lambda i: (0, 0, 0)),
            pl.BlockSpec((CHUNK, CHUNK), lambda i: (0, 0)),
            pl.BlockSpec((CHUNK, width), lambda i: (0, 0)),
        ],
        out_specs=pl.BlockSpec((rows, width), lambda i: (i, 0)),
        out_shape=jax.ShapeDtypeStruct((t, width), BF16),
        compiler_params=_params(("parallel",), blocks),
        name=name,
    )(u, vs, gates, w_mix, mask, bias_full)


ATTN_ROWS = DIL_PAIRS[-1][0]
ATTN_UNROLL = 4


def _dil_attn_kernel(q0_ref, q1_ref, q2_ref, kc_ref, vc_ref, kp_ref, vp_ref, bias_ref, gate_ref, o_ref,
                     k_scr, v_scr, og0, og1, og2, lg0, lg1, lg2):
    rows = ATTN_ROWS
    first = pl.program_id(1) == 0
    k_scr[0:rows, :] = kp_ref[...]
    k_scr[rows:2 * rows, :] = kc_ref[...]
    v_scr[0:rows, :] = vp_ref[...]
    v_scr[rows:2 * rows, :] = vc_ref[...]
    prev_cols = (lax.broadcasted_iota(jnp.int32, (1, 2 * STEPS), 1) < STEPS).astype(F32)
    ones = jnp.ones((2 * STEPS, HEAD_DIM), BF16)
    scale = HEAD_DIM ** -0.5
    for g, (q_ref, og, lg) in enumerate(((q0_ref, og0, lg0), (q1_ref, og1, lg1), (q2_ref, og2, lg2))):
        dil = DIL_PAIRS[g][1]
        span = STEPS * dil
        bias = bias_ref[g]

        def block(u, carry, q_ref=q_ref, og=og, lg=lg, dil=dil, span=span, bias=bias):
            sp = u // dil
            base = sp * span + (u - sp * dil)
            q = q_ref[pl.ds(base, STEPS, stride=dil), :].astype(BF16)
            kk = k_scr[pl.ds(rows + base - span, 2 * STEPS, stride=dil), :].astype(BF16)
            vv = v_scr[pl.ds(rows + base - span, 2 * STEPS, stride=dil), :].astype(BF16)
            no_prev = jnp.where(jnp.logical_and(first, sp == 0), NEG_INF, 0.0).astype(F32)
            logits = (lax.dot_general(q, kk, (((1,), (1,)), ((), ())), preferred_element_type=F32) * scale
                      + bias + no_prev * prev_cols)
            m = jnp.max(logits, axis=-1, keepdims=True)
            e = jnp.exp(logits - m).astype(BF16)
            os_ = jnp.dot(e, jnp.concatenate([vv, ones], axis=1), preferred_element_type=F32)
            ssum = os_[:, HEAD_DIM:]
            og[pl.ds(base, STEPS, stride=dil), :] = os_[:, :HEAD_DIM] / ssum
            lg[pl.ds(base, STEPS, stride=dil), :] = m + jnp.log(ssum)
            return carry

        lax.fori_loop(0, rows // STEPS, block, 0, unroll=ATTN_UNROLL)

    def combine(c, carry):
        rs = pl.ds(pl.multiple_of(c * 2 * STEPS, 2 * STEPS), 2 * STEPS)
        l0, l1, l2 = lg0[rs, :], lg1[rs, :], lg2[rs, :]
        mx = jnp.maximum(jnp.maximum(l0, l1), l2)
        w0, w1, w2 = jnp.exp(l0 - mx), jnp.exp(l1 - mx), jnp.exp(l2 - mx)
        mix = (w0 * og0[rs, :] + w1 * og1[rs, :] + w2 * og2[rs, :]) / (w0 + w1 + w2)
        o_ref[rs, :] = (gate_ref[rs, :].astype(F32) * mix).astype(o_ref.dtype)
        return carry

    lax.fori_loop(0, rows // (2 * STEPS), combine, 0)


def _dil_attn(q, kv, bias_tabs, gates, *, name):
    _, s, _ = q.shape
    rows = ATTN_ROWS
    assert s % rows == 0
    slab = lambda off: pl.BlockSpec((None, rows, HEAD_DIM), lambda h, n, off=off: (off + h, n, 0))
    prev = lambda off: pl.BlockSpec((None, rows, HEAD_DIM),
                                    lambda h, n, off=off: (off + h, jnp.maximum(n - 1, 0), 0))
    blocks = (2 * 7 * _nbytes((rows, HEAD_DIM), F32) + 4 * _nbytes((rows, HEAD_DIM), gates.dtype)
              + 2 * _nbytes((N_DIL, STEPS, 2 * STEPS), F32) + 10 * _nbytes((rows, HEAD_DIM), F32))
    return pl.pallas_call(
        _dil_attn_kernel,
        grid=(KV_HEADS, s // rows),
        in_specs=[
            slab(0), slab(KV_HEADS), slab(2 * KV_HEADS),
            slab(0), slab(KV_HEADS),
            prev(0), prev(KV_HEADS),
            pl.BlockSpec((N_DIL, None, STEPS, 2 * STEPS), lambda h, n: (0, h, 0, 0)),
            pl.BlockSpec((rows, HEAD_DIM), lambda h, n: (n, KV_HEADS + h)),
        ],
        out_specs=pl.BlockSpec((rows, HEAD_DIM), lambda h, n: (n, h)),
        out_shape=jax.ShapeDtypeStruct((s, KV_HEADS * HEAD_DIM), BF16),
        scratch_shapes=[pltpu.VMEM((2 * rows, HEAD_DIM), F32), pltpu.VMEM((2 * rows, HEAD_DIM), F32)]
        + [pltpu.VMEM((rows, HEAD_DIM), F32)] * 6,
        compiler_params=_params(("parallel", "parallel"), blocks),
        name=name,
    )(q, q, q, kv, kv, kv, kv, bias_tabs, gates)


def _smp_attn_kernel(q0_ref, q1_ref, q2_ref, kn_ref, vn_ref, ck_ref, cv_ref, tab_ref, gate_ref, o_ref,
                     kbuf, vbuf, *, n_past, n_new):
    n_keys = kbuf.shape[0]
    kbuf[0:n_past, :] = ck_ref[...]
    vbuf[0:n_past, :] = cv_ref[...]
    kbuf[n_past:n_past + n_new, :] = kn_ref[...]
    vbuf[n_past:n_past + n_new, :] = vn_ref[...]
    pad = n_keys - n_past - n_new
    kbuf[n_past + n_new:, :] = jnp.zeros((pad, HEAD_DIM), F32)
    vbuf[n_past + n_new:, :] = jnp.zeros((pad, HEAD_DIM), F32)
    q = jnp.concatenate([q0_ref[...].astype(F32), q1_ref[...].astype(F32), q2_ref[...].astype(F32),
                         jnp.zeros((n_new, HEAD_DIM), F32)], axis=0)
    logits = lax.dot_general(q.astype(BF16), kbuf[...].astype(BF16), (((1,), (1,)), ((), ())),
                             preferred_element_type=F32) * (HEAD_DIM ** -0.5) + tab_ref[...]
    m = jnp.max(logits, axis=-1, keepdims=True)
    e = jnp.exp(logits - m)
    s = jnp.sum(e, axis=-1, keepdims=True)
    o = jnp.dot(e.astype(BF16), vbuf[...].astype(BF16), preferred_element_type=F32) / s
    lse = m + jnp.log(s)
    ls = [lse[g * n_new:(g + 1) * n_new] for g in range(N_DIL)]
    mx = jnp.maximum(jnp.maximum(ls[0], ls[1]), ls[2])
    ws = [jnp.exp(l - mx) for l in ls]
    mix = sum(ws[g] * o[g * n_new:(g + 1) * n_new] for g in range(N_DIL)) / (ws[0] + ws[1] + ws[2])
    o_ref[...] = (gate_ref[...].astype(F32) * mix).astype(o_ref.dtype)


def _smp_attn(q, kv, cache, tab, gates, *, n_new, out_dtype, name):
    b, n_past, _ = cache.shape
    n_keys = tab.shape[-1]
    hb = lambda off: pl.BlockSpec((None, n_new, HEAD_DIM), lambda i, h, off=off: (off + h, i, 0))
    cb = lambda off: pl.BlockSpec((None, n_past, HEAD_DIM), lambda i, h, off=off: (i, 0, off + h))
    blocks = (4 * _nbytes((n_past, HEAD_DIM), F32) + 2 * _nbytes((n_keys, HEAD_DIM), F32)
              + 2 * _nbytes(tab.shape[1:], F32) + 8 * _nbytes((N_DIL * n_new, n_keys), F32))
    return pl.pallas_call(
        functools.partial(_smp_attn_kernel, n_past=n_past, n_new=n_new),
        grid=(b, KV_HEADS),
        in_specs=[
            hb(0), hb(KV_HEADS), hb(2 * KV_HEADS),
            hb(0), hb(KV_HEADS),
            cb(0), cb(KV_HEADS),
            pl.BlockSpec((None, (N_DIL + 1) * n_new, n_keys), lambda i, h: (h, 0, 0)),
            pl.BlockSpec((n_new, HEAD_DIM), lambda i, h: (i, KV_HEADS + h)),
        ],
        out_specs=pl.BlockSpec((n_new, HEAD_DIM), lambda i, h: (i, h)),
        out_shape=jax.ShapeDtypeStruct((b * n_new, KV_HEADS * HEAD_DIM), out_dtype),
        scratch_shapes=[pltpu.VMEM((n_keys, HEAD_DIM), F32), pltpu.VMEM((n_keys, HEAD_DIM), F32)],
        compiler_params=_params(("parallel", "parallel"), blocks),
        name=name,
    )(q, q, q, kv, kv, cache, cache, tab, gates)


def _mem_attn_kernel(q_ref, k_ref, v_ref, o_ref, *, head_dim):
    scale = head_dim ** -0.5
    for h in range(MEM_HEADS):
        cs = slice(h * head_dim, (h + 1) * head_dim)
        logits = lax.dot_general(q_ref[:, cs].astype(BF16), k_ref[:, cs].astype(BF16), (((1,), (1,)), ((), ())),
                                 preferred_element_type=F32) * scale
        m = jnp.max(logits, axis=-1, keepdims=True)
        e = jnp.exp(logits - m)
        s = jnp.sum(e, axis=-1, keepdims=True)
        o = jnp.dot(e.astype(BF16), v_ref[:, cs].astype(BF16), preferred_element_type=F32)
        o_ref[:, cs] = (o / s).astype(o_ref.dtype)


def _mem_attn(q, mem_kv, *, tm, per_batch, out_dtype, name):
    t, w = q.shape
    m_tok = mem_kv.shape[-2]
    if per_batch:
        kspec = pl.BlockSpec((None, m_tok, w), lambda i: (i, 0, 0))
        vspec = pl.BlockSpec((None, m_tok, w), lambda i: (i, 0, 1))
    else:
        kspec = pl.BlockSpec((m_tok, w), lambda i: (0, 0))
        vspec = pl.BlockSpec((m_tok, w), lambda i: (0, 1))
    blocks = 4 * _nbytes((tm, w), BF16) + 4 * _nbytes((m_tok, w), F32) + 6 * _nbytes((tm, m_tok), F32) \
        + 2 * _nbytes((tm, w), F32)
    return pl.pallas_call(
        functools.partial(_mem_attn_kernel, head_dim=w // MEM_HEADS),
        grid=(t // tm,),
        in_specs=[pl.BlockSpec((tm, w), lambda i: (i, 0)), kspec, vspec],
        out_specs=pl.BlockSpec((tm, w), lambda i: (i, 0)),
        out_shape=jax.ShapeDtypeStruct((t, w), out_dtype),
        compiler_params=_params(("parallel",), blocks),
        name=name,
    )(q, mem_kv, mem_kv)


def _oddeven_merge_sort_pairs(n):
    pairs = []

    def merge(lo, hi, r):
        step = r * 2
        if step < hi - lo:
            merge(lo, hi, step)
            merge(lo + r, hi, step)
            pairs.extend((i, i + r) for i in range(lo + r, hi - r, step))
        else:
            pairs.append((lo, lo + r))

    def sort(lo, hi):
        if hi - lo >= 1:
            mid = lo + (hi - lo) // 2
            sort(lo, mid)
            sort(mid + 1, hi)
            merge(lo, hi, 1)

    sort(0, n - 1)
    return pairs


def _compare_exchange(wires, i, j):
    a, b = wires[i], wires[j]
    if b is None:
        return
    if a is None:
        wires[i], wires[j] = b, None
        return
    wires[i], wires[j] = jnp.maximum(a, b), jnp.minimum(a, b)


def _sort_desc(wires):
    for i, j in _oddeven_merge_sort_pairs(len(wires)):
        _compare_exchange(wires, i, j)
    return wires


def _bitonic_merge_desc(wires):
    n = len(wires)
    dist = n // 2
    while dist >= 1:
        for i in range(n):
            if i & dist == 0:
                _compare_exchange(wires, i, i + dist)
        dist //= 2
    return wires


def _top16_desc(s):
    wires = [s[SUBLANES * i:SUBLANES * (i + 1), :] for i in range(PEER_N_KEYS // SUBLANES)]
    wires = _sort_desc(wires)
    for shift in (4, 2, 1):
        wires = [jnp.maximum(wires[i], pltpu.roll(wires[PEER_TOPK - 1 - i], shift, 0)) for i in range(PEER_TOPK)]
        wires = _bitonic_merge_desc(wires)
    return wires


_CAND_PAIRS = [(j, k) for j in range(PEER_TOPK) for k in range(PEER_TOPK) if (j + 1) * (k + 1) <= PEER_TOPK]


def _peer_scores_kernel(qt_ref, k1_ref, k2_ref, s1_ref, e1_ref, s2_ref, e2_ref, tau_ref, *, n_lane_tiles):
    for h in range(PEER_HEADS):
        r0 = h * 2 * PEER_KEY_DIM
        s1_ref[h] = jnp.dot(k1_ref[h], qt_ref[r0:r0 + PEER_KEY_DIM, :], preferred_element_type=F32)
        s2_ref[h] = jnp.dot(k2_ref[h], qt_ref[r0 + PEER_KEY_DIM:r0 + 2 * PEER_KEY_DIM, :],
                            preferred_element_type=F32)
    sub = lax.broadcasted_iota(jnp.int32, (SUBLANES, LANES), 0)

    def lane_tile(lt, carry):
        lanes = pl.ds(pl.multiple_of(lt * LANES, LANES), LANES)
        a_pack = [None] * PEER_TOPK
        b_pack = [None] * PEER_TOPK
        for h in range(PEER_HEADS):
            ta = _top16_desc(s1_ref[h, :, lanes])
            tb = _top16_desc(s2_ref[h, :, lanes])
            for j in range(PEER_TOPK):
                a_pack[j] = ta[j] if h == 0 else jnp.where(sub == h, ta[j], a_pack[j])
                b_pack[j] = tb[j] if h == 0 else jnp.where(sub == h, tb[j], b_pack[j])
        cands = [a_pack[j] + b_pack[k] for j, k in _CAND_PAIRS]
        top = cands[0]
        wires = _sort_desc(list(cands) + [None] * (64 - len(cands)))
        tau = wires[PEER_TOPK - 1]
        z = jnp.zeros_like(tau)
        for c in cands:
            z = z + jnp.where(c >= tau, jnp.exp(c - top), 0.0)
        zinv = 1.0 / z
        tau_ref[:, lanes] = tau
        for h in range(PEER_HEADS):
            e1_ref[h, :, lanes] = jnp.exp(s1_ref[h, :, lanes] - a_pack[0][h:h + 1, :])
            e2_ref[h, :, lanes] = jnp.exp(s2_ref[h, :, lanes] - b_pack[0][h:h + 1, :]) * zinv[h:h + 1, :]
        return carry

    lax.fori_loop(0, n_lane_tiles, lane_tile, 0)


def _peer_scores(qt, keys1, keys2, *, tl, name):
    d, t = qt.shape
    assert t % tl == 0 and d == PEER_HEADS * 2 * PEER_KEY_DIM
    big = jax.ShapeDtypeStruct((PEER_HEADS, PEER_N_KEYS, t), F32)
    bspec = lambda: pl.BlockSpec((PEER_HEADS, PEER_N_KEYS, tl), lambda i: (0, 0, i))
    blocks = 2 * _nbytes((d, tl), BF16) + 4 * _nbytes(keys1.shape, BF16) + 8 * _nbytes((PEER_HEADS, PEER_N_KEYS, tl), F32)
    return pl.pallas_call(
        functools.partial(_peer_scores_kernel, n_lane_tiles=tl // LANES),
        grid=(t // tl,),
        in_specs=[
            pl.BlockSpec((d, tl), lambda i: (0, i)),
            pl.BlockSpec(keys1.shape, lambda i: (0, 0, 0)),
            pl.BlockSpec(keys2.shape, lambda i: (0, 0, 0)),
        ],
        out_specs=[bspec(), bspec(), bspec(), bspec(), pl.BlockSpec((PEER_HEADS, tl), lambda i: (0, i))],
        out_shape=[big, big, big, big, jax.ShapeDtypeStruct((PEER_HEADS, t), F32)],
        compiler_params=_params(("parallel",), blocks),
        name=name,
    )(qt, keys1, keys2)


def _peer_dense_step(u_ref, v_ref, ht_ref, s1_ref, e1_ref, s2_ref, e2_ref, tau_ref, o_ref,
                     a_new, a_old, w_new, w_old, key_row_in_group, rows_per_block, n_lane_tiles):
    half = PEER_N_KEYS // 2
    d = o_ref.shape[1]

    def gate_lane_tile(lt):
        lanes = slice(lt * LANES, (lt + 1) * LANES)
        for i0 in range(0, rows_per_block, 2):
            pieces = {}
            for hf in range(2):
                krows = slice(hf * half, (hf + 1) * half)
                gates = {}
                for h in range(PEER_HEADS):
                    s1_rows = s1_ref[h, :, lanes]
                    e1_rows = e1_ref[h, :, lanes]
                    tau = tau_ref[h:h + 1, lanes]
                    s2t = s2_ref[h, krows, lanes]
                    e2t = e2_ref[h, krows, lanes]
                    for ii in (i0, i0 + 1):
                        r = key_row_in_group + ii
                        hit = (s1_rows[r:r + 1, :] + s2t) >= tau
                        term = jnp.where(hit, e1_rows[r:r + 1, :] * e2t, 0.0)
                        gates[ii] = term if h == 0 else gates[ii] + term
                for ii, gate in gates.items():
                    erows = slice(ii * PEER_N_KEYS + hf * half, ii * PEER_N_KEYS + (hf + 1) * half)
                    pieces[ii, hf] = gate * jax.nn.gelu(a_old[erows, lanes])
            for ii in (i0, i0 + 1):
                tile = jnp.concatenate([pieces[ii, 0], pieces[ii, 1]], axis=0)
                w_new[lanes, ii * PEER_N_KEYS:(ii + 1) * PEER_N_KEYS] = tile.T.astype(w_new.dtype)

    for c in range(n_lane_tiles // 2):
        lanes2 = slice(2 * c * LANES, 2 * (c + 1) * LANES)
        cols = slice(c * d // (n_lane_tiles // 2), (c + 1) * d // (n_lane_tiles // 2))
        a_new[:, lanes2] = jnp.dot(u_ref[...], ht_ref[:, lanes2], preferred_element_type=F32)
        gate_lane_tile(2 * c)
        o_ref[:, cols] += jnp.dot(w_old[...], v_ref[:, cols], preferred_element_type=F32)
        gate_lane_tile(2 * c + 1)


def _peer_dense_kernel(u_ref, v_ref, ht_ref, s1_ref, e1_ref, s2_ref, e2_ref, tau_ref, o_ref,
                       a0, a1, w0, w1, *, rows_per_block, n_lane_tiles):
    j = pl.program_id(1)

    @pl.when(j == 0)
    def _():
        o_ref[...] = jnp.zeros_like(o_ref)
        a1[...] = jnp.zeros_like(a1)
        w0[...] = jnp.zeros_like(w0)

    blocks_per_group = SUBLANES // rows_per_block
    step = functools.partial(_peer_dense_step, u_ref, v_ref, ht_ref, s1_ref, e1_ref, s2_ref, e2_ref, tau_ref,
                             o_ref, rows_per_block=rows_per_block, n_lane_tiles=n_lane_tiles)
    even = lax.rem(j, 2) == 0

    @pl.when(even)
    def _():
        step(a_new=a0, a_old=a1, w_new=w1, w_old=w0, key_row_in_group=(blocks_per_group - 1) * rows_per_block)

    @pl.when(jnp.logical_not(even))
    def _():
        step(a_new=a1, a_old=a0, w_new=w0, w_old=w1, key_row_in_group=0)


def _peer_dense(u_tab, v_tab, ht, s1, e1, s2, e2, tau, *, tm, eb, name):
    n_exp, d = u_tab.shape
    t = ht.shape[1]
    assert t % tm == 0 and tm % (2 * LANES) == 0 and n_exp % eb == 0 and eb % PEER_N_KEYS == 0
    assert eb // PEER_N_KEYS in (SUBLANES // 2, SUBLANES)
    n_blocks = n_exp // eb
    last = n_blocks - 1
    full = lambda: pl.BlockSpec((PEER_HEADS, PEER_N_KEYS, tm), lambda i, j: (0, 0, i))
    blocks_per_group = SUBLANES * PEER_N_KEYS // eb
    group = lambda: pl.BlockSpec((PEER_HEADS, SUBLANES, tm),
                                 lambda i, j: (0, jnp.clip(j - 1, 0, last) // blocks_per_group, i))
    blocks = (2 * _nbytes((eb, d), BF16) * 2 + 2 * _nbytes((d, tm), BF16) + 8 * _nbytes((PEER_HEADS, PEER_N_KEYS, tm), F32)
              + 2 * _nbytes((d, tm), F32) + 2 * _nbytes((eb, tm), F32) + 2 * _nbytes((eb, tm), BF16)
              + 2 * _nbytes((d, tm), F32))
    return pl.pallas_call(
        functools.partial(_peer_dense_kernel, rows_per_block=eb // PEER_N_KEYS, n_lane_tiles=tm // LANES),
        grid=(t // tm, n_blocks + 2),
        in_specs=[
            pl.BlockSpec((eb, d), lambda i, j: (jnp.minimum(j, last), 0)),
            pl.BlockSpec((eb, d), lambda i, j: (jnp.clip(j - 2, 0, last), 0)),
            pl.BlockSpec((d, tm), lambda i, j: (0, i)),
            group(), group(), full(), full(),
            pl.BlockSpec((PEER_HEADS, tm), lambda i, j: (0, i)),
        ],
        out_specs=pl.BlockSpec((tm, d), lambda i, j: (i, 0)),
        out_shape=jax.ShapeDtypeStruct((t, d), F32),
        scratch_shapes=[pltpu.VMEM((eb, tm), F32), pltpu.VMEM((eb, tm), F32),
                        pltpu.VMEM((tm, eb), BF16), pltpu.VMEM((tm, eb), BF16)],
        compiler_params=_params(("parallel", "arbitrary"), blocks),
        name=name,
    )(u_tab, v_tab, ht, s1, e1, s2, e2, tau)


def _final_kernel(x_ref, y_ref, g_ref, o_ref):
    x = x_ref[...] + y_ref[...]
    o_ref[...] = (x * lax.rsqrt(jnp.mean(x * x, axis=-1, keepdims=True) + EPS) * g_ref[...]).astype(o_ref.dtype)


def _final(x, y, g, *, tm, name):
    t, d = x.shape
    blocks = 6 * _nbytes((tm, d), F32) + 2 * _nbytes((tm, d), F32)
    row = lambda: pl.BlockSpec((tm, d), lambda i: (i, 0))
    return pl.pallas_call(
        _final_kernel,
        grid=(t // tm,),
        in_specs=[row(), row(), pl.BlockSpec((1, d), lambda i: (0, 0))],
        out_specs=row(),
        out_shape=jax.ShapeDtypeStruct((t, d), F32),
        compiler_params=_params(("parallel",), blocks),
        name=name,
    )(x, y, g.reshape(1, d).astype(F32))


def _t5_bucket(dist):
    max_exact = REL_BUCKETS // 2
    d = np.maximum(dist, 1).astype(np.float64)
    large = max_exact + (np.log(d / max_exact) / math.log(REL_MAX_DIST / max_exact)
                         * (REL_BUCKETS - max_exact)).astype(np.int64)
    large = np.minimum(large, REL_BUCKETS - 1)
    return np.where(dist < max_exact, dist, large).astype(np.int32)


def _spread_bias(b):
    return jnp.broadcast_to(b.T[:, :, None], (CHUNK, SGU_GROUPS, CHUNK)).reshape(CHUNK, SGU_GROUPS * CHUNK)


def _group_bias(rel_bias, group, dil):
    bucket = _t5_bucket(dil * np.arange(STEPS + 1))
    table = rel_bias[:, group * KV_HEADS:(group + 1) * KV_HEADS].astype(F32)
    runs, start = [], 0
    for i in range(1, len(bucket) + 1):
        if i == len(bucket) or bucket[i] != bucket[start]:
            runs.append(jnp.broadcast_to(table[bucket[start]:bucket[start] + 1], (i - start, KV_HEADS)))
            start = i
    return jnp.concatenate(runs, axis=0).T


def _prompt_bias_table(rel_bias, group, dil):
    period = 3 * STEPS
    bias = _group_bias(rel_bias, group, dil)
    vec = jnp.concatenate([bias[:, ::-1], jnp.full((KV_HEADS, period - STEPS - 1), NEG_INF, F32)], axis=1)
    flat = jnp.broadcast_to(vec[:, None, :], (KV_HEADS, STEPS, period)).reshape(KV_HEADS, STEPS * period)
    skew = flat[:, :STEPS * (period - 1)].reshape(KV_HEADS, STEPS, period - 1)
    return skew[:, :, :2 * STEPS]


def _sample_bias_table(rel_bias, n_past, n_new, n_keys):
    rows = []
    for g, (win, dil) in enumerate(DIL_PAIRS):
        rev = _group_bias(rel_bias, g, dil)[:, ::-1]
        for t in range(n_new):
            lo = n_past + t - win
            hi = n_keys - (n_past + t + 1)
            rows.append(lax.pad(rev, jnp.asarray(NEG_INF, F32), [(0, 0, 0), (lo, hi, dil - 1)]))
    rows += [jnp.zeros((KV_HEADS, n_keys), F32)] * n_new
    return jnp.stack(rows, axis=1)


def _input_projection(x, p, *, tm, norm_tm, act_dtype, tag):
    width = SGU_GROUPS * CHUNK
    hn = _rmsnorm(x, p["norm_mix"], tm=norm_tm, out_dtype=BF16, name=f"{tag}_norm_mix")
    mm = functools.partial(_matmul, [hn], tm=tm, tn=width)
    u = mm([p["w_u"]], out_dtype=BF16, epilogue=_epi_gelu, name=f"{tag}_proj_u")
    vs = mm([p["w_vs"]], out_dtype=F32, epilogue=_epi_gelu_layernorm,
            extras=(("row", p["sgu_ln_g"]), ("row", p["sgu_ln_b"])), name=f"{tag}_proj_vs")
    q = mm([p["w_q"]], out_dtype=F32, head_major=True, name=f"{tag}_proj_q")
    kv = mm([p["w_kv"]], out_dtype=F32, head_major=True, name=f"{tag}_proj_kv")
    gates = mm([p["w_gates"]], out_dtype=act_dtype, epilogue=_epi_sigmoid, name=f"{tag}_proj_gates")
    return u, vs, q, kv, gates


def _channel_stages(x, ya, yb, mem_kv, p, *, tm, norm_tm, tl, peer_tm, per_batch, mem_tm, act_dtype, tag):
    d = x.shape[1]
    x1 = _matmul([ya, yb], [p["w_out_a"], p["w_out_b"]], tm=tm, tn=d // 2, out_dtype=F32,
                 epilogue=_epi_residual, extras=(("full", x),), name=f"{tag}_proj_out")
    hq = _rmsnorm(x1, p["norm_mem"], tm=norm_tm, out_dtype=BF16, name=f"{tag}_norm_mem")
    qm = _matmul([hq], [p["w_mq"]], tm=tm, tn=d // 2, out_dtype=act_dtype, name=f"{tag}_mem_q")
    om = _mem_attn(qm, mem_kv, tm=mem_tm, per_batch=per_batch, out_dtype=act_dtype, name=f"{tag}_mem_attn")
    x2 = _matmul([om], [p["w_mo"]], tm=tm, tn=d // 2, out_dtype=F32, epilogue=_epi_residual,
                 extras=(("full", x1),), name=f"{tag}_mem_o")
    ht = _rmsnorm(x2, p["norm_peer"], tm=tl, out_dtype=BF16, transpose=True, name=f"{tag}_norm_peer")
    qt = _matmul([p["peer_wq_t"]], [ht], tm=d // 2, tn=tl, out_dtype=BF16, name=f"{tag}_peer_q")
    s1, e1, s2, e2, tau = _peer_scores(qt, p["peer_keys1"], p["peer_keys2"], tl=tl, name=f"{tag}_peer_scores")
    y = _peer_dense(p["peer_u"], p["peer_v"], ht, s1, e1, s2, e2, tau, tm=peer_tm, eb=512,
                    name=f"{tag}_peer_dense")
    return _final(x2, y, p["norm_final"], tm=tl, name=f"{tag}_final")


def kernel(x_prompt, x_sample, mem_prompt, cache_win, cache_mem_kv, rel_bias, norm_mix, w_in, sgu_ln_g, sgu_ln_b, sgu_w, sgu_b, w_out, norm_mem, norm_memtok, w_mq, w_mk, w_mv, w_mo, norm_peer, peer_wq, peer_keys1, peer_keys2, peer_u, peer_v, norm_final):
    depth = norm_mix.shape[0]
    batch, seq, d = x_prompt.shape
    dec_batch, dec_seq, _ = x_sample.shape
    assert depth == 1 and batch == 1, "one layer and one prompt sequence are supported"
    width = SGU_GROUPS * CHUNK
    n_past = cache_win.shape[2]
    assert seq % (DIL_PAIRS[-1][0]) == 0 and dec_seq <= CHUNK and CHUNK % dec_seq == 0

    w_in_b = w_in[0].astype(BF16)
    o_q = 2 * width
    o_k = o_q + N_DIL * width
    o_g = o_k + 2 * width
    p = dict(
        norm_mix=norm_mix[0], norm_mem=norm_mem[0], norm_peer=norm_peer[0], norm_final=norm_final,
        w_u=w_in_b[:, :width], w_vs=w_in_b[:, width:o_q], w_q=w_in_b[:, o_q:o_k], w_kv=w_in_b[:, o_k:o_g],
        w_gates=w_in_b[:, o_g:],
        sgu_ln_g=sgu_ln_g[0].reshape(1, width), sgu_ln_b=sgu_ln_b[0].reshape(1, width),
        w_out_a=w_out[0, :width].astype(BF16), w_out_b=w_out[0, width:].astype(BF16),
        w_mq=w_mq[0].astype(BF16), w_mo=w_mo[0].astype(BF16),
        peer_wq_t=peer_wq[0].T.astype(BF16),
        peer_keys1=peer_keys1[0].astype(BF16), peer_keys2=peer_keys2[0].astype(BF16),
        peer_u=peer_u[0].astype(BF16), peer_v=peer_v[0].astype(BF16),
    )

    xp = x_prompt.reshape(seq, d)
    u, vs, q, kv, gates = _input_projection(xp, p, tm=PROMPT_TM, norm_tm=NORM_TM, act_dtype=BF16, tag="p")
    tril = np.tril(np.ones((CHUNK, CHUNK), np.float32))
    bias_full = _spread_bias(sgu_b[0])
    ya = _sgu(u, vs, gates, sgu_w[0], jnp.asarray(tril), bias_full, n_chunks=4, name="p_sgu")
    bias_tabs = jnp.stack([_prompt_bias_table(rel_bias, g, dil) for g, (_, dil) in enumerate(DIL_PAIRS)])
    yb = _dil_attn(q, kv, bias_tabs, gates, name="p_dil_attn")
    mem_n = _rmsnorm(mem_prompt.reshape(-1, d), norm_memtok[0], tm=mem_prompt.shape[1], out_dtype=BF16,
                     name="p_norm_memtok")
    w_mkv = jnp.concatenate([w_mk[0], w_mv[0]], axis=1).astype(BF16)
    mem_kv = _matmul([mem_n], [w_mkv], tm=mem_n.shape[0], tn=d // 2, out_dtype=F32, name="p_mem_kv")
    y_prompt = _channel_stages(xp, ya, yb, mem_kv, p, tm=PROMPT_TM, norm_tm=NORM_TM, tl=PEER_TOKEN_TILE,
                               peer_tm=PEER_TOKEN_TILE, per_batch=False, mem_tm=PEER_TOKEN_TILE,
                               act_dtype=BF16, tag="p")
    win_rows = min(DIL_PAIRS[-1][0], seq)
    state_win_p = kv[:, seq - win_rows:].transpose(1, 0, 2).reshape(1, 1, win_rows, 2, KV_HEADS, HEAD_DIM)
    state_mem_p = mem_kv.reshape(1, 1, mem_kv.shape[0], 2, MEM_HEADS, d // MEM_HEADS)

    ts = dec_batch * dec_seq
    assert ts % CHUNK == 0
    xs = x_sample.reshape(ts, d)
    u, vs, q, kv, gates = _input_projection(xs, p, tm=ts, norm_tm=ts, act_dtype=F32, tag="s")
    reps = CHUNK // dec_seq
    blockdiag = np.kron(np.eye(reps, dtype=np.float32), np.tril(np.ones((dec_seq, dec_seq), np.float32)))
    w_small = jnp.tile(sgu_w[0][:, :dec_seq, :dec_seq], (1, reps, reps))
    bias_small = _spread_bias(jnp.tile(sgu_b[0][:, :dec_seq], (1, reps)))
    ya = _sgu(u, vs, gates, w_small, jnp.asarray(blockdiag), bias_small, n_chunks=1, name="s_sgu")
    n_keys = -(-(n_past + dec_seq) // LANES) * LANES
    tab = _sample_bias_table(rel_bias, n_past, dec_seq, n_keys)
    cache = cache_win[0].reshape(dec_batch, n_past, 2 * width)
    yb = _smp_attn(q, kv, cache, tab, gates, n_new=dec_seq, out_dtype=F32, name="s_smp_attn")
    mem_s = cache_mem_kv[0].reshape(dec_batch, cache_mem_kv.shape[2], 2 * d)
    y_sample = _channel_stages(xs, ya, yb, mem_s, p, tm=ts, norm_tm=ts, tl=ts, peer_tm=ts, per_batch=True,
                               mem_tm=dec_seq, act_dtype=F32, tag="s")
    state_win_s = kv.transpose(1, 0, 2).reshape(1, dec_batch, dec_seq, 2, KV_HEADS, HEAD_DIM)
    state_sgu_s = vs.reshape(1, dec_batch, dec_seq, width)

    return (y_prompt.reshape(batch, seq, d), y_sample.reshape(dec_batch, dec_seq, d),
            state_win_p, state_mem_p, state_win_s, state_sgu_s)
```

```python
import functools
import math

import numpy as np
import jax
import jax.numpy as jnp
from jax import lax
from jax.experimental import pallas as pl
from jax.experimental.pallas import tpu as pltpu

F32 = jnp.float32
BF16 = jnp.bfloat16

EPS = 1e-6
NEG_INF = -1e30

LANES = 128
SUBLANES = 8
VMEM_PHYSICAL_BYTES = 64 * 1024 * 1024
VMEM_COMPILER_RESERVE_BYTES = 6 * 1024 * 1024

CHUNK = 128
SGU_GROUPS = 8
HEAD_DIM = 128
KV_HEADS = 8
DIL_PAIRS = ((128, 1), (512, 4), (2048, 16))
N_DIL = len(DIL_PAIRS)
STEPS = 128
REL_BUCKETS = 32
REL_MAX_DIST = 2048
MEM_HEADS = 4
PEER_HEADS = 8
PEER_N_KEYS = 128
PEER_KEY_DIM = 128
PEER_TOPK = 16

PROMPT_TM = 1024
NORM_TM = 512
PEER_TOKEN_TILE = 512


def _vmem_limit(block_bytes):
    return int(min(block_bytes + VMEM_COMPILER_RESERVE_BYTES, VMEM_PHYSICAL_BYTES - 4 * 1024 * 1024))


def _nbytes(shape, dtype):
    return int(np.prod(shape)) * jnp.dtype(dtype).itemsize


def _params(semantics, block_bytes):
    return pltpu.CompilerParams(dimension_semantics=semantics, vmem_limit_bytes=_vmem_limit(block_bytes))


def _rmsnorm_kernel(x_ref, g_ref, o_ref, *, transpose):
    x = x_ref[...].astype(F32)
    y = x * lax.rsqrt(jnp.mean(x * x, axis=-1, keepdims=True) + EPS) * g_ref[...]
    if transpose:
        y = y.T
    o_ref[...] = y.astype(o_ref.dtype)


def _rmsnorm(x, g, *, tm, out_dtype, transpose=False, name):
    t, d = x.shape
    assert t % tm == 0
    if transpose:
        out_shape = jax.ShapeDtypeStruct((d, t), out_dtype)
        out_spec = pl.BlockSpec((d, tm), lambda i: (0, i))
    else:
        out_shape = jax.ShapeDtypeStruct((t, d), out_dtype)
        out_spec = pl.BlockSpec((tm, d), lambda i: (i, 0))
    blocks = 2 * _nbytes((tm, d), x.dtype) + 2 * _nbytes((tm, d), out_dtype) + 2 * _nbytes((tm, d), F32)
    return pl.pallas_call(
        functools.partial(_rmsnorm_kernel, transpose=transpose),
        grid=(t // tm,),
        in_specs=[pl.BlockSpec((tm, d), lambda i: (i, 0)), pl.BlockSpec((1, d), lambda i: (0, 0))],
        out_specs=out_spec,
        out_shape=out_shape,
        compiler_params=_params(("parallel",), blocks),
        name=name,
    )(x, g.reshape(1, d).astype(F32))


def _epi_cast(acc):
    return acc


def _epi_gelu(acc):
    return jax.nn.gelu(acc)


def _epi_sigmoid(acc):
    return jax.nn.sigmoid(acc)


def _epi_gelu_layernorm(acc, g, b):
    a = jax.nn.gelu(acc)
    mu = jnp.mean(a, axis=-1, keepdims=True)
    var = jnp.mean(jnp.square(a - mu), axis=-1, keepdims=True)
    return (a - mu) * lax.rsqrt(var + EPS) * g + b


def _epi_residual(acc, res):
    return res + acc


def _matmul_kernel(*refs, n_pairs, n_extras, epilogue, head_major):
    out_ref = refs[-1]
    acc = None
    for p in range(n_pairs):
        part = jnp.dot(refs[2 * p][...].astype(BF16), refs[2 * p + 1][...], preferred_element_type=F32)
        acc = part if acc is None else acc + part
    extras = [refs[2 * n_pairs + e][...] for e in range(n_extras)]
    res = epilogue(acc, *extras).astype(out_ref.dtype)
    if head_major:
        for hh in range(out_ref.shape[0]):
            out_ref[hh] = res[:, hh * HEAD_DIM:(hh + 1) * HEAD_DIM]
    else:
        out_ref[...] = res


def _matmul(xs, ws, *, tm, tn, out_dtype, epilogue=_epi_cast, extras=(), head_major=False, name):
    m = xs[0].shape[0]
    n = ws[0].shape[1]
    assert m % tm == 0 and n % tn == 0, (m, tm, n, tn)
    in_specs, args, blocks = [], [], 0
    for x, w in zip(xs, ws):
        k = x.shape[1]
        assert w.shape == (k, n) and x.shape == (m, k)
        in_specs += [pl.BlockSpec((tm, k), lambda i, j: (i, 0)), pl.BlockSpec((k, tn), lambda i, j: (0, j))]
        args += [x, w]
        blocks += 2 * _nbytes((tm, k), x.dtype) + 2 * _nbytes((k, tn), w.dtype)
    for kind, arr in extras:
        if kind == "row":
            assert arr.shape == (1, n)
            in_specs.append(pl.BlockSpec((1, tn), lambda i, j: (0, j)))
        else:
            assert kind == "full" and arr.shape == (m, n)
            in_specs.append(pl.BlockSpec((tm, tn), lambda i, j: (i, j)))
            blocks += 2 * _nbytes((tm, tn), arr.dtype)
        args.append(arr)
    blocks += 2 * _nbytes((tm, tn), out_dtype) + 2 * _nbytes((tm, tn), F32)
    if head_major:
        out_spec = pl.BlockSpec((tn // HEAD_DIM, tm, HEAD_DIM), lambda i, j: (j, i, 0))
        out_shape = jax.ShapeDtypeStruct((n // HEAD_DIM, m, HEAD_DIM), out_dtype)
    else:
        out_spec = pl.BlockSpec((tm, tn), lambda i, j: (i, j))
        out_shape = jax.ShapeDtypeStruct((m, n), out_dtype)
    return pl.pallas_call(
        functools.partial(_matmul_kernel, n_pairs=len(xs), n_extras=len(extras), epilogue=epilogue,
                          head_major=head_major),
        grid=(m // tm, n // tn),
        in_specs=in_specs,
        out_specs=out_spec,
        out_shape=out_shape,
        compiler_params=_params(("parallel", "parallel"), blocks),
        name=name,
    )(*args)


def _sgu_kernel(u_ref, vs_ref, gate_ref, w_ref, mask_ref, b_ref, o_ref, *, n_chunks):
    mask = mask_ref[...]
    for g in range(SGU_GROUPS):
        cols = slice(g * CHUNK, (g + 1) * CHUNK)
        wm = (w_ref[g] * mask).astype(BF16)
        for c in range(n_chunks):
            rows = slice(c * CHUNK, (c + 1) * CHUNK)
            mixed = jnp.dot(wm, vs_ref[rows, cols].astype(BF16), preferred_element_type=F32) + b_ref[:, cols]
            ya = u_ref[rows, cols].astype(F32) * mixed
            o_ref[rows, cols] = (gate_ref[rows, cols].astype(F32) * ya).astype(o_ref.dtype)


def _sgu(u, vs, gates, w_mix, mask, bias_full, *, n_chunks, name):
    t, width = u.shape
    rows = n_chunks * CHUNK
    assert t % rows == 0 and width == SGU_GROUPS * CHUNK
    blocks = (2 * _nbytes((rows, width), u.dtype) + 2 * _nbytes((rows, width), vs.dtype)
              + 2 * _nbytes((rows, width), gates.dtype) + 2 * _nbytes((rows, width), BF16)
              + 2 * _nbytes(w_mix.shape, F32) + 4 * _nbytes((CHUNK, width), F32))
    return pl.pallas_call(
        functools.partial(_sgu_kernel, n_chunks=n_chunks),
        grid=(t // rows,),
        in_specs=[
            pl.BlockSpec((rows, width), lambda i: (i, 0)),
            pl.BlockSpec((rows, width), lambda i: (i, 0)),
            pl.BlockSpec((rows, width), lambda i: (i, 0)),
            pl.BlockSpec((SGU_GROUPS, CHUNK, CHUNK), lambda i: (0, 0, 0)),
            pl.BlockSpec((CHUNK, CHUNK), lambda i: (0, 0)),
            pl.BlockSpec((CHUNK, width), lambda i: (0, 0)),
        ],
        out_specs=pl.BlockSpec((rows, width), lambda i: (i, 0)),
        out_shape=jax.ShapeDtypeStruct((t, width), BF16),
        compiler_params=_params(("parallel",), blocks),
        name=name,
    )(u, vs, gates, w_mix, mask, bias_full)


ATTN_ROWS = DIL_PAIRS[-1][0]
ATTN_UNROLL = 4


def _dil_attn_kernel(q0_ref, q1_ref, q2_ref, kc_ref, vc_ref, kp_ref, vp_ref, bias_ref, gate_ref, o_ref,
                     k_scr, v_scr, og0, og1, og2, lg0, lg1, lg2):
    rows = ATTN_ROWS
    first = pl.program_id(1) == 0
    k_scr[0:rows, :] = kp_ref[...]
    k_scr[rows:2 * rows, :] = kc_ref[...]
    v_scr[0:rows, :] = vp_ref[...]
    v_scr[rows:2 * rows, :] = vc_ref[...]
    prev_cols = (lax.broadcasted_iota(jnp.int32, (1, 2 * STEPS), 1) < STEPS).astype(F32)
    ones = jnp.ones((2 * STEPS, HEAD_DIM), BF16)
    scale = HEAD_DIM ** -0.5
    for g, (q_ref, og, lg) in enumerate(((q0_ref, og0, lg0), (q1_ref, og1, lg1), (q2_ref, og2, lg2))):
        dil = DIL_PAIRS[g][1]
        span = STEPS * dil
        bias = bias_ref[g]

        def block(u, carry, q_ref=q_ref, og=og, lg=lg, dil=dil, span=span, bias=bias):
            sp = u // dil
            base = sp * span + (u - sp * dil)
            q = q_ref[pl.ds(base, STEPS, stride=dil), :].astype(BF16)
            kk = k_scr[pl.ds(rows + base - span, 2 * STEPS, stride=dil), :].astype(BF16)
            vv = v_scr[pl.ds(rows + base - span, 2 * STEPS, stride=dil), :].astype(BF16)
            no_prev = jnp.where(jnp.logical_and(first, sp == 0), NEG_INF, 0.0).astype(F32)
            logits = (lax.dot_general(q, kk, (((1,), (1,)), ((), ())), preferred_element_type=F32) * scale
                      + bias + no_prev * prev_cols)
            m = jnp.max(logits, axis=-1, keepdims=True)
            e = jnp.exp(logits - m).astype(BF16)
            os_ = jnp.dot(e, jnp.concatenate([vv, ones], axis=1), preferred_element_type=F32)
            ssum = os_[:, HEAD_DIM:]
            og[pl.ds(base, STEPS, stride=dil), :] = os_[:, :HEAD_DIM] / ssum
            lg[pl.ds(base, STEPS, stride=dil), :] = m + jnp.log(ssum)
            return carry

        lax.fori_loop(0, rows // STEPS, block, 0, unroll=ATTN_UNROLL)

    def combine(c, carry):
        rs = pl.ds(pl.multiple_of(c * 2 * STEPS, 2 * STEPS), 2 * STEPS)
        l0, l1, l2 = lg0[rs, :], lg1[rs, :], lg2[rs, :]
        mx = jnp.maximum(jnp.maximum(l0, l1), l2)
        w0, w1, w2 = jnp.exp(l0 - mx), jnp.exp(l1 - mx), jnp.exp(l2 - mx)
        mix = (w0 * og0[rs, :] + w1 * og1[rs, :] + w2 * og2[rs, :]) / (w0 + w1 + w2)
        o_ref[rs, :] = (gate_ref[rs, :].astype(F32) * mix).astype(o_ref.dtype)
        return carry

    lax.fori_loop(0, rows // (2 * STEPS), combine, 0)


def _dil_attn(q, kv, bias_tabs, gates, *, name):
    _, s, _ = q.shape
    rows = ATTN_ROWS
    assert s % rows == 0
    slab = lambda off: pl.BlockSpec((None, rows, HEAD_DIM), lambda h, n, off=off: (off + h, n, 0))
    prev = lambda off: pl.BlockSpec((None, rows, HEAD_DIM),
                                    lambda h, n, off=off: (off + h, jnp.maximum(n - 1, 0), 0))
    blocks = (2 * 7 * _nbytes((rows, HEAD_DIM), F32) + 4 * _nbytes((rows, HEAD_DIM), gates.dtype)
              + 2 * _nbytes((N_DIL, STEPS, 2 * STEPS), F32) + 10 * _nbytes((rows, HEAD_DIM), F32))
    return pl.pallas_call(
        _dil_attn_kernel,
        grid=(KV_HEADS, s // rows),
        in_specs=[
            slab(0), slab(KV_HEADS), slab(2 * KV_HEADS),
            slab(0), slab(KV_HEADS),
            prev(0), prev(KV_HEADS),
            pl.BlockSpec((N_DIL, None, STEPS, 2 * STEPS), lambda h, n: (0, h, 0, 0)),
            pl.BlockSpec((rows, HEAD_DIM), lambda h, n: (n, KV_HEADS + h)),
        ],
        out_specs=pl.BlockSpec((rows, HEAD_DIM), lambda h, n: (n, h)),
        out_shape=jax.ShapeDtypeStruct((s, KV_HEADS * HEAD_DIM), BF16),
        scratch_shapes=[pltpu.VMEM((2 * rows, HEAD_DIM), F32), pltpu.VMEM((2 * rows, HEAD_DIM), F32)]
        + [pltpu.VMEM((rows, HEAD_DIM), F32)] * 6,
        compiler_params=_params(("parallel", "parallel"), blocks),
        name=name,
    )(q, q, q, kv, kv, kv, kv, bias_tabs, gates)


def _smp_attn_kernel(q_ref, kvn_ref, cache_ref, tab_ref, gate_ref, o_ref, kbuf, vbuf, *, n_past, n_new):
    n_keys = kbuf.shape[0]
    pad = n_keys - n_past - n_new
    kbuf[n_past + n_new:, :] = jnp.zeros((pad, HEAD_DIM), F32)
    vbuf[n_past + n_new:, :] = jnp.zeros((pad, HEAD_DIM), F32)
    rows_per_pos = 2 * KV_HEADS

    def head(h, carry):
        kbuf[0:n_past, :] = cache_ref[pl.ds(h, n_past, stride=rows_per_pos), :]
        vbuf[0:n_past, :] = cache_ref[pl.ds(KV_HEADS + h, n_past, stride=rows_per_pos), :]
        kbuf[n_past:n_past + n_new, :] = kvn_ref[h]
        vbuf[n_past:n_past + n_new, :] = kvn_ref[KV_HEADS + h]
        q = jnp.concatenate([q_ref[h], q_ref[KV_HEADS + h], q_ref[2 * KV_HEADS + h],
                             jnp.zeros((n_new, HEAD_DIM), F32)], axis=0)
        logits = lax.dot_general(q.astype(BF16), kbuf[...].astype(BF16), (((1,), (1,)), ((), ())),
                                 preferred_element_type=F32) * (HEAD_DIM ** -0.5) + tab_ref[h]
        m = jnp.max(logits, axis=-1, keepdims=True)
        e = jnp.exp(logits - m)
        s = jnp.sum(e, axis=-1, keepdims=True)
        o = jnp.dot(e.astype(BF16), vbuf[...].astype(BF16), preferred_element_type=F32) / s
        lse = m + jnp.log(s)
        ls = [lse[g * n_new:(g + 1) * n_new] for g in range(N_DIL)]
        mx = jnp.maximum(jnp.maximum(ls[0], ls[1]), ls[2])
        ws = [jnp.exp(l - mx) for l in ls]
        mix = sum(ws[g] * o[g * n_new:(g + 1) * n_new] for g in range(N_DIL)) / (ws[0] + ws[1] + ws[2])
        cols = pl.ds(pl.multiple_of(h * HEAD_DIM, HEAD_DIM), HEAD_DIM)
        o_ref[:, cols] = (gate_ref[:, cols].astype(F32) * mix).astype(o_ref.dtype)
        return carry

    lax.fori_loop(0, KV_HEADS, head, 0)


def _smp_attn(q, kv, cache, tab, gates, *, n_new, out_dtype, name):
    b, cache_rows, _ = cache.shape
    n_past = cache_rows // (2 * KV_HEADS)
    n_keys = tab.shape[-1]
    w = KV_HEADS * HEAD_DIM
    blocks = (2 * _nbytes((cache_rows, HEAD_DIM), F32) + 2 * _nbytes((n_keys, HEAD_DIM), F32)
              + 2 * _nbytes(tab.shape, F32) + 8 * _nbytes(((N_DIL + 1) * n_new, n_keys), F32))
    return pl.pallas_call(
        functools.partial(_smp_attn_kernel, n_past=n_past, n_new=n_new),
        grid=(b,),
        in_specs=[
            pl.BlockSpec((N_DIL * KV_HEADS, n_new, HEAD_DIM), lambda i: (0, i, 0)),
            pl.BlockSpec((2 * KV_HEADS, n_new, HEAD_DIM), lambda i: (0, i, 0)),
            pl.BlockSpec((None, cache_rows, HEAD_DIM), lambda i: (i, 0, 0)),
            pl.BlockSpec(tab.shape, lambda i: (0, 0, 0)),
            pl.BlockSpec((n_new, w), lambda i: (i, 1)),
        ],
        out_specs=pl.BlockSpec((n_new, w), lambda i: (i, 0)),
        out_shape=jax.ShapeDtypeStruct((b * n_new, w), out_dtype),
        scratch_shapes=[pltpu.VMEM((n_keys, HEAD_DIM), F32), pltpu.VMEM((n_keys, HEAD_DIM), F32)],
        compiler_params=_params(("parallel",), blocks),
        name=name,
    )(q, kv, cache, tab, gates)


def _mem_attn_kernel(q_ref, k_ref, v_ref, o_ref, *, head_dim):
    scale = head_dim ** -0.5
    for h in range(q_ref.shape[1] // head_dim):
        cs = slice(h * head_dim, (h + 1) * head_dim)
        logits = lax.dot_general(q_ref[:, cs].astype(BF16), k_ref[:, cs].astype(BF16), (((1,), (1,)), ((), ())),
                                 preferred_element_type=F32) * scale
        m = jnp.max(logits, axis=-1, keepdims=True)
        e = jnp.exp(logits - m)
        s = jnp.sum(e, axis=-1, keepdims=True)
        o = jnp.dot(e.astype(BF16), v_ref[:, cs].astype(BF16), preferred_element_type=F32)
        o_ref[:, cs] = (o / s).astype(o_ref.dtype)


def _mem_attn(q, mem_kv, *, tm, per_batch, out_dtype, name):
    t, w = q.shape
    m_tok = mem_kv.shape[-2]
    if per_batch:
        kspec = pl.BlockSpec((None, m_tok, w), lambda i: (i, 0, 0))
        vspec = pl.BlockSpec((None, m_tok, w), lambda i: (i, 0, 1))
    else:
        kspec = pl.BlockSpec((m_tok, w), lambda i: (0, 0))
        vspec = pl.BlockSpec((m_tok, w), lambda i: (0, 1))
    blocks = 4 * _nbytes((tm, w), BF16) + 4 * _nbytes((m_tok, w), F32) + 6 * _nbytes((tm, m_tok), F32) \
        + 2 * _nbytes((tm, w), F32)
    return pl.pallas_call(
        functools.partial(_mem_attn_kernel, head_dim=w // MEM_HEADS),
        grid=(t // tm,),
        in_specs=[pl.BlockSpec((tm, w), lambda i: (i, 0)), kspec, vspec],
        out_specs=pl.BlockSpec((tm, w), lambda i: (i, 0)),
        out_shape=jax.ShapeDtypeStruct((t, w), out_dtype),
        compiler_params=_params(("parallel",), blocks),
        name=name,
    )(q, mem_kv, mem_kv)


def _oddeven_merge_sort_pairs(n):
    pairs = []

    def merge(lo, hi, r):
        step = r * 2
        if step < hi - lo:
            merge(lo, hi, step)
            merge(lo + r, hi, step)
            pairs.extend((i, i + r) for i in range(lo + r, hi - r, step))
        else:
            pairs.append((lo, lo + r))

    def sort(lo, hi):
        if hi - lo >= 1:
            mid = lo + (hi - lo) // 2
            sort(lo, mid)
            sort(mid + 1, hi)
            merge(lo, hi, 1)

    sort(0, n - 1)
    return pairs


def _compare_exchange(wires, i, j):
    a, b = wires[i], wires[j]
    if b is None:
        return
    if a is None:
        wires[i], wires[j] = b, None
        return
    wires[i], wires[j] = jnp.maximum(a, b), jnp.minimum(a, b)


def _sort_desc(wires):
    for i, j in _oddeven_merge_sort_pairs(len(wires)):
        _compare_exchange(wires, i, j)
    return wires


def _bitonic_merge_desc(wires):
    n = len(wires)
    dist = n // 2
    while dist >= 1:
        for i in range(n):
            if i & dist == 0:
                _compare_exchange(wires, i, i + dist)
        dist //= 2
    return wires


def _top16_desc(s):
    wires = [s[SUBLANES * i:SUBLANES * (i + 1), :] for i in range(PEER_N_KEYS // SUBLANES)]
    wires = _sort_desc(wires)
    for shift in (4, 2, 1):
        wires = [jnp.maximum(wires[i], pltpu.roll(wires[PEER_TOPK - 1 - i], shift, 0)) for i in range(PEER_TOPK)]
        wires = _bitonic_merge_desc(wires)
    return wires


_CAND_PAIRS = [(j, k) for j in range(PEER_TOPK) for k in range(PEER_TOPK) if (j + 1) * (k + 1) <= PEER_TOPK]


def _peer_scores_kernel(qt_ref, k1_ref, k2_ref, s1_ref, e1_ref, s2_ref, e2_ref, tau_ref, *, n_lane_tiles):
    for h in range(PEER_HEADS):
        r0 = h * 2 * PEER_KEY_DIM
        s1_ref[h] = jnp.dot(k1_ref[h], qt_ref[r0:r0 + PEER_KEY_DIM, :], preferred_element_type=F32)
        s2_ref[h] = jnp.dot(k2_ref[h], qt_ref[r0 + PEER_KEY_DIM:r0 + 2 * PEER_KEY_DIM, :],
                            preferred_element_type=F32)
    sub = lax.broadcasted_iota(jnp.int32, (SUBLANES, LANES), 0)

    def lane_tile(lt, carry):
        lanes = pl.ds(pl.multiple_of(lt * LANES, LANES), LANES)
        a_pack = [None] * PEER_TOPK
        b_pack = [None] * PEER_TOPK
        for h in range(PEER_HEADS):
            ta = _top16_desc(s1_ref[h, :, lanes])
            tb = _top16_desc(s2_ref[h, :, lanes])
            for j in range(PEER_TOPK):
                a_pack[j] = ta[j] if h == 0 else jnp.where(sub == h, ta[j], a_pack[j])
                b_pack[j] = tb[j] if h == 0 else jnp.where(sub == h, tb[j], b_pack[j])
        cands = [a_pack[j] + b_pack[k] for j, k in _CAND_PAIRS]
        top = cands[0]
        wires = _sort_desc(list(cands) + [None] * (64 - len(cands)))
        tau = wires[PEER_TOPK - 1]
        z = jnp.zeros_like(tau)
        for c in cands:
            z = z + jnp.where(c >= tau, jnp.exp(c - top), 0.0)
        zinv = 1.0 / z
        tau_ref[:, lanes] = tau
        for h in range(PEER_HEADS):
            e1_ref[h, :, lanes] = jnp.exp(s1_ref[h, :, lanes] - a_pack[0][h:h + 1, :])
            e2_ref[h, :, lanes] = jnp.exp(s2_ref[h, :, lanes] - b_pack[0][h:h + 1, :]) * zinv[h:h + 1, :]
        return carry

    lax.fori_loop(0, n_lane_tiles, lane_tile, 0)


def _peer_scores(qt, keys1, keys2, *, tl, name):
    d, t = qt.shape
    assert t % tl == 0 and d == PEER_HEADS * 2 * PEER_KEY_DIM
    big = jax.ShapeDtypeStruct((PEER_HEADS, PEER_N_KEYS, t), F32)
    bspec = lambda: pl.BlockSpec((PEER_HEADS, PEER_N_KEYS, tl), lambda i: (0, 0, i))
    blocks = 2 * _nbytes((d, tl), BF16) + 4 * _nbytes(keys1.shape, BF16) + 8 * _nbytes((PEER_HEADS, PEER_N_KEYS, tl), F32)
    return pl.pallas_call(
        functools.partial(_peer_scores_kernel, n_lane_tiles=tl // LANES),
        grid=(t // tl,),
        in_specs=[
            pl.BlockSpec((d, tl), lambda i: (0, i)),
            pl.BlockSpec(keys1.shape, lambda i: (0, 0, 0)),
            pl.BlockSpec(keys2.shape, lambda i: (0, 0, 0)),
        ],
        out_specs=[bspec(), bspec(), bspec(), bspec(), pl.BlockSpec((PEER_HEADS, tl), lambda i: (0, i))],
        out_shape=[big, big, big, big, jax.ShapeDtypeStruct((PEER_HEADS, t), F32)],
        compiler_params=_params(("parallel",), blocks),
        name=name,
    )(qt, keys1, keys2)


def _peer_dense_step(u_ref, v_ref, ht_ref, s1_ref, e1_ref, s2_ref, e2_ref, tau_ref, o_ref,
                     a_new, a_old, w_new, w_old, key_row_in_group, rows_per_block, n_lane_tiles):
    half = PEER_N_KEYS // 2
    d = o_ref.shape[1]

    def gate_lane_tile(lt):
        lanes = slice(lt * LANES, (lt + 1) * LANES)
        for i0 in range(0, rows_per_block, 2):
            pieces = {}
            for hf in range(2):
                krows = slice(hf * half, (hf + 1) * half)
                gates = {}
                for h in range(PEER_HEADS):
                    s1_rows = s1_ref[h, :, lanes]
                    e1_rows = e1_ref[h, :, lanes]
                    tau = tau_ref[h:h + 1, lanes]
                    s2t = s2_ref[h, krows, lanes]
                    e2t = e2_ref[h, krows, lanes]
                    for ii in (i0, i0 + 1):
                        r = key_row_in_group + ii
                        hit = (s1_rows[r:r + 1, :] + s2t) >= tau
                        term = jnp.where(hit, e1_rows[r:r + 1, :] * e2t, 0.0)
                        gates[ii] = term if h == 0 else gates[ii] + term
                for ii, gate in gates.items():
                    erows = slice(ii * PEER_N_KEYS + hf * half, ii * PEER_N_KEYS + (hf + 1) * half)
                    pieces[ii, hf] = gate * jax.nn.gelu(a_old[erows, lanes])
            for ii in (i0, i0 + 1):
                tile = jnp.concatenate([pieces[ii, 0], pieces[ii, 1]], axis=0)
                w_new[lanes, ii * PEER_N_KEYS:(ii + 1) * PEER_N_KEYS] = tile.T.astype(w_new.dtype)

    for c in range(n_lane_tiles // 2):
        lanes2 = slice(2 * c * LANES, 2 * (c + 1) * LANES)
        cols = slice(c * d // (n_lane_tiles // 2), (c + 1) * d // (n_lane_tiles // 2))
        a_new[:, lanes2] = jnp.dot(u_ref[...], ht_ref[:, lanes2], preferred_element_type=F32)
        gate_lane_tile(2 * c)
        o_ref[:, cols] += jnp.dot(w_old[...], v_ref[:, cols], preferred_element_type=F32)
        gate_lane_tile(2 * c + 1)


def _peer_dense_kernel(u_ref, v_ref, ht_ref, s1_ref, e1_ref, s2_ref, e2_ref, tau_ref, o_ref,
                       a0, a1, w0, w1, *, rows_per_block, n_lane_tiles):
    j = pl.program_id(1)

    @pl.when(j == 0)
    def _():
        o_ref[...] = jnp.zeros_like(o_ref)
        a1[...] = jnp.zeros_like(a1)
        w0[...] = jnp.zeros_like(w0)

    blocks_per_group = SUBLANES // rows_per_block
    step = functools.partial(_peer_dense_step, u_ref, v_ref, ht_ref, s1_ref, e1_ref, s2_ref, e2_ref, tau_ref,
                             o_ref, rows_per_block=rows_per_block, n_lane_tiles=n_lane_tiles)
    even = lax.rem(j, 2) == 0

    @pl.when(even)
    def _():
        step(a_new=a0, a_old=a1, w_new=w1, w_old=w0, key_row_in_group=(blocks_per_group - 1) * rows_per_block)

    @pl.when(jnp.logical_not(even))
    def _():
        step(a_new=a1, a_old=a0, w_new=w0, w_old=w1, key_row_in_group=0)


def _peer_dense(u_tab, v_tab, ht, s1, e1, s2, e2, tau, *, tm, eb, name):
    n_exp, d = u_tab.shape
    t = ht.shape[1]
    assert t % tm == 0 and tm % (2 * LANES) == 0 and n_exp % eb == 0 and eb % PEER_N_KEYS == 0
    assert eb // PEER_N_KEYS in (SUBLANES // 2, SUBLANES)
    n_blocks = n_exp // eb
    last = n_blocks - 1
    full = lambda: pl.BlockSpec((PEER_HEADS, PEER_N_KEYS, tm), lambda i, j: (0, 0, i))
    blocks_per_group = SUBLANES * PEER_N_KEYS // eb
    group = lambda: pl.BlockSpec((PEER_HEADS, SUBLANES, tm),
                                 lambda i, j: (0, jnp.clip(j - 1, 0, last) // blocks_per_group, i))
    blocks = (2 * _nbytes((eb, d), BF16) * 2 + 2 * _nbytes((d, tm), BF16) + 8 * _nbytes((PEER_HEADS, PEER_N_KEYS, tm), F32)
              + 2 * _nbytes((d, tm), F32) + 2 * _nbytes((eb, tm), F32) + 2 * _nbytes((eb, tm), BF16)
              + 2 * _nbytes((d, tm), F32))
    return pl.pallas_call(
        functools.partial(_peer_dense_kernel, rows_per_block=eb // PEER_N_KEYS, n_lane_tiles=tm // LANES),
        grid=(t // tm, n_blocks + 2),
        in_specs=[
            pl.BlockSpec((eb, d), lambda i, j: (jnp.minimum(j, last), 0)),
            pl.BlockSpec((eb, d), lambda i, j: (jnp.clip(j - 2, 0, last), 0)),
            pl.BlockSpec((d, tm), lambda i, j: (0, i)),
            group(), group(), full(), full(),
            pl.BlockSpec((PEER_HEADS, tm), lambda i, j: (0, i)),
        ],
        out_specs=pl.BlockSpec((tm, d), lambda i, j: (i, 0)),
        out_shape=jax.ShapeDtypeStruct((t, d), F32),
        scratch_shapes=[pltpu.VMEM((eb, tm), F32), pltpu.VMEM((eb, tm), F32),
                        pltpu.VMEM((tm, eb), BF16), pltpu.VMEM((tm, eb), BF16)],
        compiler_params=_params(("parallel", "arbitrary"), blocks),
        name=name,
    )(u_tab, v_tab, ht, s1, e1, s2, e2, tau)


def _final_kernel(x_ref, y_ref, g_ref, o_ref):
    x = x_ref[...] + y_ref[...]
    o_ref[...] = (x * lax.rsqrt(jnp.mean(x * x, axis=-1, keepdims=True) + EPS) * g_ref[...]).astype(o_ref.dtype)


def _final(x, y, g, *, tm, name):
    t, d = x.shape
    blocks = 6 * _nbytes((tm, d), F32) + 2 * _nbytes((tm, d), F32)
    row = lambda: pl.BlockSpec((tm, d), lambda i: (i, 0))
    return pl.pallas_call(
        _final_kernel,
        grid=(t // tm,),
        in_specs=[row(), row(), pl.BlockSpec((1, d), lambda i: (0, 0))],
        out_specs=row(),
        out_shape=jax.ShapeDtypeStruct((t, d), F32),
        compiler_params=_params(("parallel",), blocks),
        name=name,
    )(x, y, g.reshape(1, d).astype(F32))


def _t5_bucket(dist):
    max_exact = REL_BUCKETS // 2
    d = np.maximum(dist, 1).astype(np.float64)
    large = max_exact + (np.log(d / max_exact) / math.log(REL_MAX_DIST / max_exact)
                         * (REL_BUCKETS - max_exact)).astype(np.int64)
    large = np.minimum(large, REL_BUCKETS - 1)
    return np.where(dist < max_exact, dist, large).astype(np.int32)


def _spread_bias(b):
    return jnp.broadcast_to(b.T[:, :, None], (CHUNK, SGU_GROUPS, CHUNK)).reshape(CHUNK, SGU_GROUPS * CHUNK)


def _group_bias(rel_bias, group, dil):
    bucket = _t5_bucket(dil * np.arange(STEPS + 1))
    table = rel_bias[:, group * KV_HEADS:(group + 1) * KV_HEADS].astype(F32)
    runs, start = [], 0
    for i in range(1, len(bucket) + 1):
        if i == len(bucket) or bucket[i] != bucket[start]:
            runs.append(jnp.broadcast_to(table[bucket[start]:bucket[start] + 1], (i - start, KV_HEADS)))
            start = i
    return jnp.concatenate(runs, axis=0).T


def _prompt_bias_table(rel_bias, group, dil):
    period = 3 * STEPS
    bias = _group_bias(rel_bias, group, dil)
    vec = jnp.concatenate([bias[:, ::-1], jnp.full((KV_HEADS, period - STEPS - 1), NEG_INF, F32)], axis=1)
    flat = jnp.broadcast_to(vec[:, None, :], (KV_HEADS, STEPS, period)).reshape(KV_HEADS, STEPS * period)
    skew = flat[:, :STEPS * (period - 1)].reshape(KV_HEADS, STEPS, period - 1)
    return skew[:, :, :2 * STEPS]


def _sample_bias_table(rel_bias, n_past, n_new, n_keys):
    rows = []
    for g, (win, dil) in enumerate(DIL_PAIRS):
        rev = _group_bias(rel_bias, g, dil)[:, ::-1]
        for t in range(n_new):
            lo = n_past + t - win
            hi = n_keys - (n_past + t + 1)
            rows.append(lax.pad(rev, jnp.asarray(NEG_INF, F32), [(0, 0, 0), (lo, hi, dil - 1)]))
    rows += [jnp.zeros((KV_HEADS, n_keys), F32)] * n_new
    return jnp.stack(rows, axis=1)


def _input_projection(x, p, *, tm, norm_tm, act_dtype, tag):
    width = SGU_GROUPS * CHUNK
    hn = _rmsnorm(x, p["norm_mix"], tm=norm_tm, out_dtype=BF16, name=f"{tag}_norm_mix")
    mm = functools.partial(_matmul, [hn], tm=tm, tn=width)
    u = mm([p["w_u"]], out_dtype=BF16, epilogue=_epi_gelu, name=f"{tag}_proj_u")
    vs = mm([p["w_vs"]], out_dtype=F32, epilogue=_epi_gelu_layernorm,
            extras=(("row", p["sgu_ln_g"]), ("row", p["sgu_ln_b"])), name=f"{tag}_proj_vs")
    q = mm([p["w_q"]], out_dtype=F32, head_major=True, name=f"{tag}_proj_q")
    kv = mm([p["w_kv"]], out_dtype=F32, head_major=True, name=f"{tag}_proj_kv")
    gates = mm([p["w_gates"]], out_dtype=act_dtype, epilogue=_epi_sigmoid, name=f"{tag}_proj_gates")
    return u, vs, q, kv, gates


def _channel_stages(x, ya, yb, mem_kv, p, *, tm, norm_tm, tl, peer_tm, per_batch, mem_tm, act_dtype, tag):
    d = x.shape[1]
    x1 = _matmul([ya, yb], [p["w_out_a"], p["w_out_b"]], tm=tm, tn=d // 2, out_dtype=F32,
                 epilogue=_epi_residual, extras=(("full", x),), name=f"{tag}_proj_out")
    hq = _rmsnorm(x1, p["norm_mem"], tm=norm_tm, out_dtype=BF16, name=f"{tag}_norm_mem")
    qm = _matmul([hq], [p["w_mq"]], tm=tm, tn=d // 2, out_dtype=act_dtype, name=f"{tag}_mem_q")
    om = _mem_attn(qm, mem_kv, tm=mem_tm, per_batch=per_batch, out_dtype=act_dtype, name=f"{tag}_mem_attn")
    x2 = _matmul([om], [p["w_mo"]], tm=tm, tn=d // 2, out_dtype=F32, epilogue=_epi_residual,
                 extras=(("full", x1),), name=f"{tag}_mem_o")
    ht = _rmsnorm(x2, p["norm_peer"], tm=tl, out_dtype=BF16, transpose=True, name=f"{tag}_norm_peer")
    qt = _matmul([p["peer_wq_t"]], [ht], tm=d // 2, tn=tl, out_dtype=BF16, name=f"{tag}_peer_q")
    s1, e1, s2, e2, tau = _peer_scores(qt, p["peer_keys1"], p["peer_keys2"], tl=tl, name=f"{tag}_peer_scores")
    y = _peer_dense(p["peer_u"], p["peer_v"], ht, s1, e1, s2, e2, tau, tm=peer_tm, eb=512,
                    name=f"{tag}_peer_dense")
    return _final(x2, y, p["norm_final"], tm=tl, name=f"{tag}_final")


def kernel(x_prompt, x_sample, mem_prompt, cache_win, cache_mem_kv, rel_bias, norm_mix, w_in, sgu_ln_g, sgu_ln_b, sgu_w, sgu_b, w_out, norm_mem, norm_memtok, w_mq, w_mk, w_mv, w_mo, norm_peer, peer_wq, peer_keys1, peer_keys2, peer_u, peer_v, norm_final):
    depth = norm_mix.shape[0]
    batch, seq, d = x_prompt.shape
    dec_batch, dec_seq, _ = x_sample.shape
    assert depth == 1 and batch == 1, "one layer and one prompt sequence are supported"
    width = SGU_GROUPS * CHUNK
    n_past = cache_win.shape[2]
    assert seq % (DIL_PAIRS[-1][0]) == 0 and dec_seq <= CHUNK and CHUNK % dec_seq == 0

    w_in_b = w_in[0].astype(BF16)
    o_q = 2 * width
    o_k = o_q + N_DIL * width
    o_g = o_k + 2 * width
    p = dict(
        norm_mix=norm_mix[0], norm_mem=norm_mem[0], norm_peer=norm_peer[0], norm_final=norm_final,
        w_u=w_in_b[:, :width], w_vs=w_in_b[:, width:o_q], w_q=w_in_b[:, o_q:o_k], w_kv=w_in_b[:, o_k:o_g],
        w_gates=w_in_b[:, o_g:],
        sgu_ln_g=sgu_ln_g[0].reshape(1, width), sgu_ln_b=sgu_ln_b[0].reshape(1, width),
        w_out_a=w_out[0, :width].astype(BF16), w_out_b=w_out[0, width:].astype(BF16),
        w_mq=w_mq[0].astype(BF16), w_mo=w_mo[0].astype(BF16),
        peer_wq_t=peer_wq[0].T.astype(BF16),
        peer_keys1=peer_keys1[0].astype(BF16), peer_keys2=peer_keys2[0].astype(BF16),
        peer_u=peer_u[0].astype(BF16), peer_v=peer_v[0].astype(BF16),
    )

    xp = x_prompt.reshape(seq, d)
    u, vs, q, kv, gates = _input_projection(xp, p, tm=PROMPT_TM, norm_tm=NORM_TM, act_dtype=BF16, tag="p")
    tril = np.tril(np.ones((CHUNK, CHUNK), np.float32))
    bias_full = _spread_bias(sgu_b[0])
    ya = _sgu(u, vs, gates, sgu_w[0], jnp.asarray(tril), bias_full, n_chunks=4, name="p_sgu")
    bias_tabs = jnp.stack([_prompt_bias_table(rel_bias, g, dil) for g, (_, dil) in enumerate(DIL_PAIRS)])
    yb = _dil_attn(q, kv, bias_tabs, gates, name="p_dil_attn")
    mem_n = _rmsnorm(mem_prompt.reshape(-1, d), norm_memtok[0], tm=mem_prompt.shape[1], out_dtype=BF16,
                     name="p_norm_memtok")
    w_mkv = jnp.concatenate([w_mk[0], w_mv[0]], axis=1).astype(BF16)
    mem_kv = _matmul([mem_n], [w_mkv], tm=mem_n.shape[0], tn=d // 2, out_dtype=F32, name="p_mem_kv")
    y_prompt = _channel_stages(xp, ya, yb, mem_kv, p, tm=PROMPT_TM, norm_tm=NORM_TM, tl=PEER_TOKEN_TILE,
                               peer_tm=PEER_TOKEN_TILE, per_batch=False, mem_tm=PEER_TOKEN_TILE,
                               act_dtype=BF16, tag="p")
    win_rows = min(DIL_PAIRS[-1][0], seq)
    state_win_p = kv[:, seq - win_rows:].transpose(1, 0, 2).reshape(1, 1, win_rows, 2, KV_HEADS, HEAD_DIM)
    state_mem_p = mem_kv.reshape(1, 1, mem_kv.shape[0], 2, MEM_HEADS, d // MEM_HEADS)

    ts = dec_batch * dec_seq
    assert ts % CHUNK == 0
    xs = x_sample.reshape(ts, d)
    u, vs, q, kv, gates = _input_projection(xs, p, tm=ts, norm_tm=ts, act_dtype=F32, tag="s")
    reps = CHUNK // dec_seq
    blockdiag = np.kron(np.eye(reps, dtype=np.float32), np.tril(np.ones((dec_seq, dec_seq), np.float32)))
    w_small = jnp.tile(sgu_w[0][:, :dec_seq, :dec_seq], (1, reps, reps))
    bias_small = _spread_bias(jnp.tile(sgu_b[0][:, :dec_seq], (1, reps)))
    ya = _sgu(u, vs, gates, w_small, jnp.asarray(blockdiag), bias_small, n_chunks=1, name="s_sgu")
    n_keys = -(-(n_past + dec_seq) // LANES) * LANES
    tab = _sample_bias_table(rel_bias, n_past, dec_seq, n_keys)
    cache = cache_win[0].reshape(dec_batch, n_past * 2 * KV_HEADS, HEAD_DIM)
    yb = _smp_attn(q, kv, cache, tab, gates, n_new=dec_seq, out_dtype=F32, name="s_smp_attn")
    mem_s = cache_mem_kv[0].reshape(dec_batch, cache_mem_kv.shape[2], 2 * d)
    y_sample = _channel_stages(xs, ya, yb, mem_s, p, tm=ts, norm_tm=ts, tl=ts, peer_tm=ts, per_batch=True,
                               mem_tm=dec_seq, act_dtype=F32, tag="s")
    state_win_s = kv.transpose(1, 0, 2).reshape(1, dec_batch, dec_seq, 2, KV_HEADS, HEAD_DIM)
    state_sgu_s = vs.reshape(1, dec_batch, dec_seq, width)

    return (y_prompt.reshape(batch, seq, d), y_sample.reshape(dec_batch, dec_seq, d),
            state_win_p, state_mem_p, state_win_s, state_sgu_s)
```

```python
import functools
import math

import numpy as np
import jax
import jax.numpy as jnp
from jax import lax
from jax.experimental import pallas as pl
from jax.experimental.pallas import tpu as pltpu

F32 = jnp.float32
BF16 = jnp.bfloat16

EPS = 1e-6
NEG_INF = -1e30

LANES = 128
SUBLANES = 8
VMEM_PHYSICAL_BYTES = 64 * 1024 * 1024
VMEM_COMPILER_RESERVE_BYTES = 6 * 1024 * 1024

CHUNK = 128
SGU_GROUPS = 8
HEAD_DIM = 128
KV_HEADS = 8
DIL_PAIRS = ((128, 1), (512, 4), (2048, 16))
N_DIL = len(DIL_PAIRS)
STEPS = 128
REL_BUCKETS = 32
REL_MAX_DIST = 2048
MEM_HEADS = 4
PEER_HEADS = 8
PEER_N_KEYS = 128
PEER_KEY_DIM = 128
PEER_TOPK = 16

PROMPT_TM = 1024
NORM_TM = 512
PEER_TOKEN_TILE = 512


def _vmem_limit(block_bytes):
    return int(min(block_bytes + VMEM_COMPILER_RESERVE_BYTES, VMEM_PHYSICAL_BYTES - 4 * 1024 * 1024))


def _nbytes(shape, dtype):
    return int(np.prod(shape)) * jnp.dtype(dtype).itemsize


def _params(semantics, block_bytes):
    return pltpu.CompilerParams(dimension_semantics=semantics, vmem_limit_bytes=_vmem_limit(block_bytes))


def _rmsnorm_kernel(x_ref, g_ref, o_ref, *, transpose):
    x = x_ref[...].astype(F32)
    y = x * lax.rsqrt(jnp.mean(x * x, axis=-1, keepdims=True) + EPS) * g_ref[...]
    if transpose:
        y = y.T
    o_ref[...] = y.astype(o_ref.dtype)


def _rmsnorm(x, g, *, tm, out_dtype, transpose=False, name):
    t, d = x.shape
    assert t % tm == 0
    if transpose:
        out_shape = jax.ShapeDtypeStruct((d, t), out_dtype)
        out_spec = pl.BlockSpec((d, tm), lambda i: (0, i))
    else:
        out_shape = jax.ShapeDtypeStruct((t, d), out_dtype)
        out_spec = pl.BlockSpec((tm, d), lambda i: (i, 0))
    blocks = 2 * _nbytes((tm, d), x.dtype) + 2 * _nbytes((tm, d), out_dtype) + 2 * _nbytes((tm, d), F32)
    return pl.pallas_call(
        functools.partial(_rmsnorm_kernel, transpose=transpose),
        grid=(t // tm,),
        in_specs=[pl.BlockSpec((tm, d), lambda i: (i, 0)), pl.BlockSpec((1, d), lambda i: (0, 0))],
        out_specs=out_spec,
        out_shape=out_shape,
        compiler_params=_params(("parallel",), blocks),
        name=name,
    )(x, g.reshape(1, d).astype(F32))


def _epi_cast(acc):
    return acc


def _epi_gelu(acc):
    return jax.nn.gelu(acc)


def _epi_sigmoid(acc):
    return jax.nn.sigmoid(acc)


def _epi_gelu_layernorm(acc, g, b):
    a = jax.nn.gelu(acc)
    mu = jnp.mean(a, axis=-1, keepdims=True)
    var = jnp.mean(jnp.square(a - mu), axis=-1, keepdims=True)
    return (a - mu) * lax.rsqrt(var + EPS) * g + b


def _epi_residual(acc, res):
    return res + acc


def _matmul_kernel(*refs, n_pairs, n_extras, epilogue, head_major):
    out_ref = refs[-1]
    acc = None
    for p in range(n_pairs):
        part = jnp.dot(refs[2 * p][...].astype(BF16), refs[2 * p + 1][...], preferred_element_type=F32)
        acc = part if acc is None else acc + part
    extras = [refs[2 * n_pairs + e][...] for e in range(n_extras)]
    res = epilogue(acc, *extras).astype(out_ref.dtype)
    if head_major:
        for hh in range(out_ref.shape[0]):
            out_ref[hh] = res[:, hh * HEAD_DIM:(hh + 1) * HEAD_DIM]
    else:
        out_ref[...] = res


def _matmul(xs, ws, *, tm, tn, out_dtype, epilogue=_epi_cast, extras=(), head_major=False, name):
    m = xs[0].shape[0]
    n = ws[0].shape[1]
    assert m % tm == 0 and n % tn == 0, (m, tm, n, tn)
    in_specs, args, blocks = [], [], 0
    for x, w in zip(xs, ws):
        k = x.shape[1]
        assert w.shape == (k, n) and x.shape == (m, k)
        in_specs += [pl.BlockSpec((tm, k), lambda i, j: (i, 0)), pl.BlockSpec((k, tn), lambda i, j: (0, j))]
        args += [x, w]
        blocks += 2 * _nbytes((tm, k), x.dtype) + 2 * _nbytes((k, tn), w.dtype)
    for kind, arr in extras:
        if kind == "row":
            assert arr.shape == (1, n)
            in_specs.append(pl.BlockSpec((1, tn), lambda i, j: (0, j)))
        else:
            assert kind == "full" and arr.shape == (m, n)
            in_specs.append(pl.BlockSpec((tm, tn), lambda i, j: (i, j)))
            blocks += 2 * _nbytes((tm, tn), arr.dtype)
        args.append(arr)
    blocks += 2 * _nbytes((tm, tn), out_dtype) + 2 * _nbytes((tm, tn), F32)
    if head_major:
        out_spec = pl.BlockSpec((tn // HEAD_DIM, tm, HEAD_DIM), lambda i, j: (j, i, 0))
        out_shape = jax.ShapeDtypeStruct((n // HEAD_DIM, m, HEAD_DIM), out_dtype)
    else:
        out_spec = pl.BlockSpec((tm, tn), lambda i, j: (i, j))
        out_shape = jax.ShapeDtypeStruct((m, n), out_dtype)
    return pl.pallas_call(
        functools.partial(_matmul_kernel, n_pairs=len(xs), n_extras=len(extras), epilogue=epilogue,
                          head_major=head_major),
        grid=(m // tm, n // tn),
        in_specs=in_specs,
        out_specs=out_spec,
        out_shape=out_shape,
        compiler_params=_params(("parallel", "parallel"), blocks),
        name=name,
    )(*args)


def _sgu_kernel(u_ref, vs_ref, gate_ref, w_ref, mask_ref, b_ref, o_ref, *, n_chunks):
    mask = mask_ref[...]
    for g in range(SGU_GROUPS):
        cols = slice(g * CHUNK, (g + 1) * CHUNK)
        wm = (w_ref[g] * mask).astype(BF16)
        for c in range(n_chunks):
            rows = slice(c * CHUNK, (c + 1) * CHUNK)
            mixed = jnp.dot(wm, vs_ref[rows, cols].astype(BF16), preferred_element_type=F32) + b_ref[:, cols]
            ya = u_ref[rows, cols].astype(F32) * mixed
            o_ref[rows, cols] = (gate_ref[rows, cols].astype(F32) * ya).astype(o_ref.dtype)


def _sgu(u, vs, gates, w_mix, mask, bias_full, *, n_chunks, name):
    t, width = u.shape
    rows = n_chunks * CHUNK
    assert t % rows == 0 and width == SGU_GROUPS * CHUNK
    blocks = (2 * _nbytes((rows, width), u.dtype) + 2 * _nbytes((rows, width), vs.dtype)
              + 2 * _nbytes((rows, width), gates.dtype) + 2 * _nbytes((rows, width), BF16)
              + 2 * _nbytes(w_mix.shape, F32) + 4 * _nbytes((CHUNK, width), F32))
    return pl.pallas_call(
        functools.partial(_sgu_kernel, n_chunks=n_chunks),
        grid=(t // rows,),
        in_specs=[
            pl.BlockSpec((rows, width), lambda i: (i, 0)),
            pl.BlockSpec((rows, width), lambda i: (i, 0)),
            pl.BlockSpec((rows, width), lambda i: (i, 0)),
            pl.BlockSpec((SGU_GROUPS, CHUNK, CHUNK), lambda i: (0, 0, 0)),
            pl.BlockSpec((CHUNK, CHUNK), lambda i: (0, 0)),
            pl.BlockSpec((CHUNK, width), lambda i: (0, 0)),
        ],
        out_specs=pl.BlockSpec((rows, width), lambda i: (i, 0)),
        out_shape=jax.ShapeDtypeStruct((t, width), BF16),
        compiler_params=_params(("parallel",), blocks),
        name=name,
    )(u, vs, gates, w_mix, mask, bias_full)


ATTN_ROWS = DIL_PAIRS[-1][0]
ATTN_UNROLL = 4


def _dil_attn_kernel(q0_ref, q1_ref, q2_ref, kc_ref, vc_ref, kp_ref, vp_ref, bias_ref, gate_ref, o_ref,
                     k_scr, v_scr, og0, og1, og2, lg0, lg1, lg2):
    rows = ATTN_ROWS
    first = pl.program_id(1) == 0
    k_scr[0:rows, :] = kp_ref[...]
    k_scr[rows:2 * rows, :] = kc_ref[...]
    v_scr[0:rows, :] = vp_ref[...]
    v_scr[rows:2 * rows, :] = vc_ref[...]
    prev_cols = (lax.broadcasted_iota(jnp.int32, (1, 2 * STEPS), 1) < STEPS).astype(F32)
    ones = jnp.ones((2 * STEPS, HEAD_DIM), BF16)
    scale = HEAD_DIM ** -0.5
    for g, (q_ref, og, lg) in enumerate(((q0_ref, og0, lg0), (q1_ref, og1, lg1), (q2_ref, og2, lg2))):
        dil = DIL_PAIRS[g][1]
        span = STEPS * dil
        bias = bias_ref[g]

        def block(u, carry, q_ref=q_ref, og=og, lg=lg, dil=dil, span=span, bias=bias):
            sp = u // dil
            base = sp * span + (u - sp * dil)
            q = q_ref[pl.ds(base, STEPS, stride=dil), :].astype(BF16)
            kk = k_scr[pl.ds(rows + base - span, 2 * STEPS, stride=dil), :].astype(BF16)
            vv = v_scr[pl.ds(rows + base - span, 2 * STEPS, stride=dil), :].astype(BF16)
            no_prev = jnp.where(jnp.logical_and(first, sp == 0), NEG_INF, 0.0).astype(F32)
            logits = (lax.dot_general(q, kk, (((1,), (1,)), ((), ())), preferred_element_type=F32) * scale
                      + bias + no_prev * prev_cols)
            m = jnp.max(logits, axis=-1, keepdims=True)
            e = jnp.exp(logits - m).astype(BF16)
            os_ = jnp.dot(e, jnp.concatenate([vv, ones], axis=1), preferred_element_type=F32)
            ssum = os_[:, HEAD_DIM:]
            og[pl.ds(base, STEPS, stride=dil), :] = os_[:, :HEAD_DIM] / ssum
            lg[pl.ds(base, STEPS, stride=dil), :] = m + jnp.log(ssum)
            return carry

        lax.fori_loop(0, rows // STEPS, block, 0, unroll=ATTN_UNROLL)

    def combine(c, carry):
        rs = pl.ds(pl.multiple_of(c * 2 * STEPS, 2 * STEPS), 2 * STEPS)
        l0, l1, l2 = lg0[rs, :], lg1[rs, :], lg2[rs, :]
        mx = jnp.maximum(jnp.maximum(l0, l1), l2)
        w0, w1, w2 = jnp.exp(l0 - mx), jnp.exp(l1 - mx), jnp.exp(l2 - mx)
        mix = (w0 * og0[rs, :] + w1 * og1[rs, :] + w2 * og2[rs, :]) / (w0 + w1 + w2)
        o_ref[rs, :] = (gate_ref[rs, :].astype(F32) * mix).astype(o_ref.dtype)
        return carry

    lax.fori_loop(0, rows // (2 * STEPS), combine, 0)


def _dil_attn(q, kv, bias_tabs, gates, *, name):
    _, s, _ = q.shape
    rows = ATTN_ROWS
    assert s % rows == 0
    slab = lambda off: pl.BlockSpec((None, rows, HEAD_DIM), lambda h, n, off=off: (off + h, n, 0))
    prev = lambda off: pl.BlockSpec((None, rows, HEAD_DIM),
                                    lambda h, n, off=off: (off + h, jnp.maximum(n - 1, 0), 0))
    blocks = (2 * 7 * _nbytes((rows, HEAD_DIM), F32) + 4 * _nbytes((rows, HEAD_DIM), gates.dtype)
              + 2 * _nbytes((N_DIL, STEPS, 2 * STEPS), F32) + 10 * _nbytes((rows, HEAD_DIM), F32))
    return pl.pallas_call(
        _dil_attn_kernel,
        grid=(KV_HEADS, s // rows),
        in_specs=[
            slab(0), slab(KV_HEADS), slab(2 * KV_HEADS),
            slab(0), slab(KV_HEADS),
            prev(0), prev(KV_HEADS),
            pl.BlockSpec((N_DIL, None, STEPS, 2 * STEPS), lambda h, n: (0, h, 0, 0)),
            pl.BlockSpec((rows, HEAD_DIM), lambda h, n: (n, KV_HEADS + h)),
        ],
        out_specs=pl.BlockSpec((rows, HEAD_DIM), lambda h, n: (n, h)),
        out_shape=jax.ShapeDtypeStruct((s, KV_HEADS * HEAD_DIM), BF16),
        scratch_shapes=[pltpu.VMEM((2 * rows, HEAD_DIM), F32), pltpu.VMEM((2 * rows, HEAD_DIM), F32)]
        + [pltpu.VMEM((rows, HEAD_DIM), F32)] * 6,
        compiler_params=_params(("parallel", "parallel"), blocks),
        name=name,
    )(q, q, q, kv, kv, kv, kv, bias_tabs, gates)


def _smp_attn_kernel(q_ref, kvn_ref, cache_ref, tab_ref, gate_ref, o_ref, kbuf, vbuf, *, n_past, n_new):
    n_keys = kbuf.shape[0]
    pad = n_keys - n_past - n_new
    kbuf[n_past + n_new:, :] = jnp.zeros((pad, HEAD_DIM), F32)
    vbuf[n_past + n_new:, :] = jnp.zeros((pad, HEAD_DIM), F32)
    rows_per_pos = 2 * KV_HEADS

    def head(h, carry):
        kbuf[0:n_past, :] = cache_ref[pl.ds(h, n_past, stride=rows_per_pos), :]
        vbuf[0:n_past, :] = cache_ref[pl.ds(KV_HEADS + h, n_past, stride=rows_per_pos), :]
        kbuf[n_past:n_past + n_new, :] = kvn_ref[h]
        vbuf[n_past:n_past + n_new, :] = kvn_ref[KV_HEADS + h]
        q = jnp.concatenate([q_ref[h], q_ref[KV_HEADS + h], q_ref[2 * KV_HEADS + h],
                             jnp.zeros((n_new, HEAD_DIM), F32)], axis=0)
        logits = lax.dot_general(q.astype(BF16), kbuf[...].astype(BF16), (((1,), (1,)), ((), ())),
                                 preferred_element_type=F32) * (HEAD_DIM ** -0.5) + tab_ref[h]
        m = jnp.max(logits, axis=-1, keepdims=True)
        e = jnp.exp(logits - m)
        s = jnp.sum(e, axis=-1, keepdims=True)
        o = jnp.dot(e.astype(BF16), vbuf[...].astype(BF16), preferred_element_type=F32) / s
        lse = m + jnp.log(s)
        ls = [lse[g * n_new:(g + 1) * n_new] for g in range(N_DIL)]
        mx = jnp.maximum(jnp.maximum(ls[0], ls[1]), ls[2])
        ws = [jnp.exp(l - mx) for l in ls]
        mix = sum(ws[g] * o[g * n_new:(g + 1) * n_new] for g in range(N_DIL)) / (ws[0] + ws[1] + ws[2])
        cols = pl.ds(pl.multiple_of(h * HEAD_DIM, HEAD_DIM), HEAD_DIM)
        o_ref[:, cols] = (gate_ref[:, cols].astype(F32) * mix).astype(o_ref.dtype)
        return carry

    lax.fori_loop(0, KV_HEADS, head, 0)


def _smp_attn(q, kv, cache, tab, gates, *, n_new, out_dtype, name):
    b, cache_rows, _ = cache.shape
    n_past = cache_rows // (2 * KV_HEADS)
    n_keys = tab.shape[-1]
    w = KV_HEADS * HEAD_DIM
    blocks = (2 * _nbytes((cache_rows, HEAD_DIM), F32) + 2 * _nbytes((n_keys, HEAD_DIM), F32)
              + 2 * _nbytes(tab.shape, F32) + 8 * _nbytes(((N_DIL + 1) * n_new, n_keys), F32))
    return pl.pallas_call(
        functools.partial(_smp_attn_kernel, n_past=n_past, n_new=n_new),
        grid=(b,),
        in_specs=[
            pl.BlockSpec((N_DIL * KV_HEADS, n_new, HEAD_DIM), lambda i: (0, i, 0)),
            pl.BlockSpec((2 * KV_HEADS, n_new, HEAD_DIM), lambda i: (0, i, 0)),
            pl.BlockSpec((None, cache_rows, HEAD_DIM), lambda i: (i, 0, 0)),
            pl.BlockSpec(tab.shape, lambda i: (0, 0, 0)),
            pl.BlockSpec((n_new, w), lambda i: (i, 1)),
        ],
        out_specs=pl.BlockSpec((n_new, w), lambda i: (i, 0)),
        out_shape=jax.ShapeDtypeStruct((b * n_new, w), out_dtype),
        scratch_shapes=[pltpu.VMEM((n_keys, HEAD_DIM), F32), pltpu.VMEM((n_keys, HEAD_DIM), F32)],
        compiler_params=_params(("parallel",), blocks),
        name=name,
    )(q, kv, cache, tab, gates)


def _mem_attn_kernel(q_ref, k_ref, v_ref, o_ref, *, head_dim):
    scale = head_dim ** -0.5
    for h in range(q_ref.shape[1] // head_dim):
        cs = slice(h * head_dim, (h + 1) * head_dim)
        logits = lax.dot_general(q_ref[:, cs].astype(BF16), k_ref[:, cs].astype(BF16), (((1,), (1,)), ((), ())),
                                 preferred_element_type=F32) * scale
        m = jnp.max(logits, axis=-1, keepdims=True)
        e = jnp.exp(logits - m)
        s = jnp.sum(e, axis=-1, keepdims=True)
        o = jnp.dot(e.astype(BF16), v_ref[:, cs].astype(BF16), preferred_element_type=F32)
        o_ref[:, cs] = (o / s).astype(o_ref.dtype)


def _mem_attn(q, mem_kv, *, tm, per_batch, out_dtype, name):
    t, w = q.shape
    m_tok = mem_kv.shape[-2]
    if per_batch:
        kspec = pl.BlockSpec((None, m_tok, w), lambda i: (i, 0, 0))
        vspec = pl.BlockSpec((None, m_tok, w), lambda i: (i, 0, 1))
    else:
        kspec = pl.BlockSpec((m_tok, w), lambda i: (0, 0))
        vspec = pl.BlockSpec((m_tok, w), lambda i: (0, 1))
    blocks = 4 * _nbytes((tm, w), BF16) + 4 * _nbytes((m_tok, w), F32) + 6 * _nbytes((tm, m_tok), F32) \
        + 2 * _nbytes((tm, w), F32)
    return pl.pallas_call(
        functools.partial(_mem_attn_kernel, head_dim=w // MEM_HEADS),
        grid=(t // tm,),
        in_specs=[pl.BlockSpec((tm, w), lambda i: (i, 0)), kspec, vspec],
        out_specs=pl.BlockSpec((tm, w), lambda i: (i, 0)),
        out_shape=jax.ShapeDtypeStruct((t, w), out_dtype),
        compiler_params=_params(("parallel",), blocks),
        name=name,
    )(q, mem_kv, mem_kv)


def _oddeven_merge_sort_pairs(n):
    pairs = []

    def merge(lo, hi, r):
        step = r * 2
        if step < hi - lo:
            merge(lo, hi, step)
            merge(lo + r, hi, step)
            pairs.extend((i, i + r) for i in range(lo + r, hi - r, step))
        else:
            pairs.append((lo, lo + r))

    def sort(lo, hi):
        if hi - lo >= 1:
            mid = lo + (hi - lo) // 2
            sort(lo, mid)
            sort(mid + 1, hi)
            merge(lo, hi, 1)

    sort(0, n - 1)
    return pairs


def _compare_exchange(wires, i, j):
    a, b = wires[i], wires[j]
    if b is None:
        return
    if a is None:
        wires[i], wires[j] = b, None
        return
    wires[i], wires[j] = jnp.maximum(a, b), jnp.minimum(a, b)


def _sort_desc(wires):
    for i, j in _oddeven_merge_sort_pairs(len(wires)):
        _compare_exchange(wires, i, j)
    return wires


def _bitonic_merge_desc(wires):
    n = len(wires)
    dist = n // 2
    while dist >= 1:
        for i in range(n):
            if i & dist == 0:
                _compare_exchange(wires, i, i + dist)
        dist //= 2
    return wires


def _top16_desc(s):
    wires = [s[SUBLANES * i:SUBLANES * (i + 1), :] for i in range(PEER_N_KEYS // SUBLANES)]
    wires = _sort_desc(wires)
    for shift in (4, 2, 1):
        wires = [jnp.maximum(wires[i], pltpu.roll(wires[PEER_TOPK - 1 - i], shift, 0)) for i in range(PEER_TOPK)]
        wires = _bitonic_merge_desc(wires)
    return wires


_CAND_PAIRS = [(j, k) for j in range(PEER_TOPK) for k in range(PEER_TOPK) if (j + 1) * (k + 1) <= PEER_TOPK]


def _peer_scores_kernel(qt_ref, k1_ref, k2_ref, cnt_ref, e1_ref, rank_ref, e2_ref, s1_ref, s2_ref,
                        *, n_lane_tiles):
    for h in range(PEER_HEADS):
        r0 = h * 2 * PEER_KEY_DIM
        s1_ref[h] = jnp.dot(k1_ref[h], qt_ref[r0:r0 + PEER_KEY_DIM, :], preferred_element_type=F32)
        s2_ref[h] = jnp.dot(k2_ref[h], qt_ref[r0 + PEER_KEY_DIM:r0 + 2 * PEER_KEY_DIM, :],
                            preferred_element_type=F32)
    sub = lax.broadcasted_iota(jnp.int32, (SUBLANES, LANES), 0)

    def lane_tile(lt, carry):
        lanes = pl.ds(pl.multiple_of(lt * LANES, LANES), LANES)
        a_pack = [None] * PEER_TOPK
        b_pack = [None] * PEER_TOPK
        for h in range(PEER_HEADS):
            ta = _top16_desc(s1_ref[h, :, lanes])
            tb = _top16_desc(s2_ref[h, :, lanes])
            for j in range(PEER_TOPK):
                a_pack[j] = ta[j] if h == 0 else jnp.where(sub == h, ta[j], a_pack[j])
                b_pack[j] = tb[j] if h == 0 else jnp.where(sub == h, tb[j], b_pack[j])
        cands = [a_pack[j] + b_pack[k] for j, k in _CAND_PAIRS]
        top = cands[0]
        wires = _sort_desc(list(cands) + [None] * (64 - len(cands)))
        tau = wires[PEER_TOPK - 1]
        z = jnp.zeros_like(tau)
        for c in cands:
            z = z + jnp.where(c >= tau, jnp.exp(c - top), 0.0)
        zinv = 1.0 / z
        for h in range(PEER_HEADS):
            s1 = s1_ref[h, :, lanes]
            s2 = s2_ref[h, :, lanes]
            tau_h = tau[h:h + 1, :]
            cnt = jnp.zeros_like(s1)
            rank = jnp.zeros_like(s2)
            for k in range(PEER_TOPK):
                b_k = b_pack[k][h:h + 1, :]
                cnt = cnt + jnp.where((s1 + b_k) >= tau_h, 1.0, 0.0)
                rank = rank + jnp.where(b_k > s2, 1.0, 0.0)
            cnt_ref[h, :, lanes] = cnt
            rank_ref[h, :, lanes] = rank.astype(rank_ref.dtype)
            e1_ref[h, :, lanes] = jnp.exp(s1 - a_pack[0][h:h + 1, :])
            e2_ref[h, :, lanes] = (jnp.exp(s2 - b_pack[0][h:h + 1, :]) * zinv[h:h + 1, :]).astype(e2_ref.dtype)
        return carry

    lax.fori_loop(0, n_lane_tiles, lane_tile, 0)


def _peer_scores(qt, keys1, keys2, *, tl, name):
    d, t = qt.shape
    assert t % tl == 0 and d == PEER_HEADS * 2 * PEER_KEY_DIM
    first = jax.ShapeDtypeStruct((PEER_HEADS, PEER_N_KEYS, t), F32)
    second = jax.ShapeDtypeStruct((PEER_HEADS, PEER_N_KEYS, t), BF16)
    bspec = lambda: pl.BlockSpec((PEER_HEADS, PEER_N_KEYS, tl), lambda i: (0, 0, i))
    blocks = 2 * _nbytes((d, tl), BF16) + 4 * _nbytes(keys1.shape, BF16) + 8 * _nbytes((PEER_HEADS, PEER_N_KEYS, tl), F32)
    return pl.pallas_call(
        functools.partial(_peer_scores_kernel, n_lane_tiles=tl // LANES),
        grid=(t // tl,),
        in_specs=[
            pl.BlockSpec((d, tl), lambda i: (0, i)),
            pl.BlockSpec(keys1.shape, lambda i: (0, 0, 0)),
            pl.BlockSpec(keys2.shape, lambda i: (0, 0, 0)),
        ],
        out_specs=[bspec(), bspec(), bspec(), bspec()],
        out_shape=[first, first, second, second],
        scratch_shapes=[pltpu.VMEM((PEER_HEADS, PEER_N_KEYS, tl), F32)] * 2,
        compiler_params=_params(("parallel",), blocks),
        name=name,
    )(qt, keys1, keys2)


def _peer_dense_step(u_ref, v_ref, ht_ref, cnt_ref, e1_ref, rank_ref, e2_ref, o_ref,
                     a_new, a_old, w_new, w_old, key_row_in_group, rows_per_block, n_lane_tiles,
                     activate=True, gate=True, accumulate=True):
    half = PEER_N_KEYS // 2
    d = o_ref.shape[1]
    bf16_rows = 2 * SUBLANES

    def gate_lane_tile(lt):
        lanes = slice(lt * LANES, (lt + 1) * LANES)
        for i0 in range(0, rows_per_block, 2):
            gates = {}
            for h in range(PEER_HEADS):
                cnt_rows = cnt_ref[h, :, lanes]
                e1_rows = e1_ref[h, :, lanes]
                rank = rank_ref[h, :, lanes]
                e2t = e2_ref[h, :, lanes]
                for ii in (i0, i0 + 1):
                    r = key_row_in_group + ii
                    cnt_b = jnp.broadcast_to(cnt_rows[r:r + 1, :], (PEER_N_KEYS, LANES)).astype(BF16)
                    e1_b = jnp.broadcast_to(e1_rows[r:r + 1, :], (PEER_N_KEYS, LANES)).astype(BF16)
                    hit = jnp.minimum(jnp.maximum(cnt_b - rank, 0), 1)
                    term = (hit * e1_b) * e2t
                    gates[ii] = term if h == 0 else gates[ii] + term
            for ii in (i0, i0 + 1):
                erows = slice(ii * PEER_N_KEYS, (ii + 1) * PEER_N_KEYS)
                tile = gates[ii].astype(F32) * jax.nn.gelu(a_old[erows, lanes])
                w_new[lanes, erows] = tile.T.astype(w_new.dtype)

    n_slices = n_lane_tiles // 2
    for c in range(n_slices):
        lanes2 = slice(2 * c * LANES, 2 * (c + 1) * LANES)
        cols = slice(c * d // n_slices, (c + 1) * d // n_slices)
        if activate:
            a_new[:, lanes2] = jnp.dot(u_ref[...], ht_ref[:, lanes2], preferred_element_type=F32)
        if gate:
            gate_lane_tile(2 * c)
        if accumulate:
            o_ref[:, cols] += jnp.dot(w_old[...], v_ref[:, cols], preferred_element_type=F32)
        if gate:
            gate_lane_tile(2 * c + 1)


def _peer_dense_kernel(u_ref, v_ref, ht_ref, cnt_ref, e1_ref, rank_ref, e2_ref, o_ref,
                       a0, a1, w0, w1, *, rows_per_block, n_lane_tiles, n_blocks):
    j = pl.program_id(1)

    @pl.when(j == 0)
    def _():
        o_ref[...] = jnp.zeros_like(o_ref)

    blocks_per_group = SUBLANES // rows_per_block
    step = functools.partial(_peer_dense_step, u_ref, v_ref, ht_ref, cnt_ref, e1_ref, rank_ref, e2_ref,
                             o_ref, rows_per_block=rows_per_block, n_lane_tiles=n_lane_tiles)
    even_step = functools.partial(step, a_new=a0, a_old=a1, w_new=w1, w_old=w0,
                                  key_row_in_group=(blocks_per_group - 1) * rows_per_block)
    odd_step = functools.partial(step, a_new=a1, a_old=a0, w_new=w0, w_old=w1, key_row_in_group=0)
    even = lax.rem(j, 2) == 0
    filling = j < 2
    draining = j >= n_blocks
    steady = jnp.logical_not(jnp.logical_or(filling, draining))

    pl.when(jnp.logical_and(even, steady))(even_step)
    pl.when(jnp.logical_and(jnp.logical_not(even), steady))(odd_step)
    pl.when(j == 0)(functools.partial(even_step, gate=False, accumulate=False))
    pl.when(j == 1)(functools.partial(odd_step, accumulate=False))
    pl.when(j == n_blocks)(functools.partial(even_step, activate=False))
    pl.when(j == n_blocks + 1)(functools.partial(odd_step, activate=False, gate=False))


def _peer_dense(u_tab, v_tab, ht, cnt, e1, rank, e2, *, tm, eb, name):
    n_exp, d = u_tab.shape
    t = ht.shape[1]
    assert t % tm == 0 and tm % (2 * LANES) == 0 and n_exp % eb == 0 and eb % PEER_N_KEYS == 0
    assert eb // PEER_N_KEYS in (SUBLANES // 2, SUBLANES)
    n_blocks = n_exp // eb
    assert n_blocks % 2 == 0
    last = n_blocks - 1
    full = lambda: pl.BlockSpec((PEER_HEADS, PEER_N_KEYS, tm), lambda i, j: (0, 0, i))
    blocks_per_group = SUBLANES * PEER_N_KEYS // eb
    group = lambda: pl.BlockSpec((PEER_HEADS, SUBLANES, tm),
                                 lambda i, j: (0, jnp.clip(j - 1, 0, last) // blocks_per_group, i))
    blocks = (2 * _nbytes((eb, d), BF16) * 2 + 2 * _nbytes((d, tm), BF16)
              + 4 * _nbytes((PEER_HEADS, PEER_N_KEYS, tm), BF16) + 4 * _nbytes((PEER_HEADS, SUBLANES, tm), F32)
              + 2 * _nbytes((d, tm), F32) + 2 * _nbytes((eb, tm), F32) + 2 * _nbytes((eb, tm), BF16)
              + 2 * _nbytes((d, tm), F32))
    return pl.pallas_call(
        functools.partial(_peer_dense_kernel, rows_per_block=eb // PEER_N_KEYS, n_lane_tiles=tm // LANES,
                          n_blocks=n_blocks),
        grid=(t // tm, n_blocks + 2),
        in_specs=[
            pl.BlockSpec((eb, d), lambda i, j: (jnp.minimum(j, last), 0)),
            pl.BlockSpec((eb, d), lambda i, j: (jnp.clip(j - 2, 0, last), 0)),
            pl.BlockSpec((d, tm), lambda i, j: (0, i)),
            group(), group(), full(), full(),
        ],
        out_specs=pl.BlockSpec((tm, d), lambda i, j: (i, 0)),
        out_shape=jax.ShapeDtypeStruct((t, d), F32),
        scratch_shapes=[pltpu.VMEM((eb, tm), F32), pltpu.VMEM((eb, tm), F32),
                        pltpu.VMEM((tm, eb), BF16), pltpu.VMEM((tm, eb), BF16)],
        compiler_params=_params(("parallel", "arbitrary"), blocks),
        name=name,
    )(u_tab, v_tab, ht, cnt, e1, rank, e2)


def _final_kernel(x_ref, y_ref, g_ref, o_ref):
    x = x_ref[...] + y_ref[...]
    o_ref[...] = (x * lax.rsqrt(jnp.mean(x * x, axis=-1, keepdims=True) + EPS) * g_ref[...]).astype(o_ref.dtype)


def _final(x, y, g, *, tm, name):
    t, d = x.shape
    blocks = 6 * _nbytes((tm, d), F32) + 2 * _nbytes((tm, d), F32)
    row = lambda: pl.BlockSpec((tm, d), lambda i: (i, 0))
    return pl.pallas_call(
        _final_kernel,
        grid=(t // tm,),
        in_specs=[row(), row(), pl.BlockSpec((1, d), lambda i: (0, 0))],
        out_specs=row(),
        out_shape=jax.ShapeDtypeStruct((t, d), F32),
        compiler_params=_params(("parallel",), blocks),
        name=name,
    )(x, y, g.reshape(1, d).astype(F32))


def _t5_bucket(dist):
    max_exact = REL_BUCKETS // 2
    d = np.maximum(dist, 1).astype(np.float64)
    large = max_exact + (np.log(d / max_exact) / math.log(REL_MAX_DIST / max_exact)
                         * (REL_BUCKETS - max_exact)).astype(np.int64)
    large = np.minimum(large, REL_BUCKETS - 1)
    return np.where(dist < max_exact, dist, large).astype(np.int32)


def _spread_bias(b):
    return jnp.broadcast_to(b.T[:, :, None], (CHUNK, SGU_GROUPS, CHUNK)).reshape(CHUNK, SGU_GROUPS * CHUNK)


def _group_bias(rel_bias, group, dil):
    bucket = _t5_bucket(dil * np.arange(STEPS + 1))
    table = rel_bias[:, group * KV_HEADS:(group + 1) * KV_HEADS].astype(F32)
    runs, start = [], 0
    for i in range(1, len(bucket) + 1):
        if i == len(bucket) or bucket[i] != bucket[start]:
            runs.append(jnp.broadcast_to(table[bucket[start]:bucket[start] + 1], (i - start, KV_HEADS)))
            start = i
    return jnp.concatenate(runs, axis=0).T


def _prompt_bias_table(rel_bias, group, dil):
    period = 3 * STEPS
    bias = _group_bias(rel_bias, group, dil)
    vec = jnp.concatenate([bias[:, ::-1], jnp.full((KV_HEADS, period - STEPS - 1), NEG_INF, F32)], axis=1)
    flat = jnp.broadcast_to(vec[:, None, :], (KV_HEADS, STEPS, period)).reshape(KV_HEADS, STEPS * period)
    skew = flat[:, :STEPS * (period - 1)].reshape(KV_HEADS, STEPS, period - 1)
    return skew[:, :, :2 * STEPS]


def _sample_bias_table(rel_bias, n_past, n_new, n_keys):
    rows = []
    for g, (win, dil) in enumerate(DIL_PAIRS):
        rev = _group_bias(rel_bias, g, dil)[:, ::-1]
        for t in range(n_new):
            lo = n_past + t - win
            hi = n_keys - (n_past + t + 1)
            rows.append(lax.pad(rev, jnp.asarray(NEG_INF, F32), [(0, 0, 0), (lo, hi, dil - 1)]))
    rows += [jnp.zeros((KV_HEADS, n_keys), F32)] * n_new
    return jnp.stack(rows, axis=1)


def _input_projection(x, p, *, tm, norm_tm, act_dtype, tag):
    width = SGU_GROUPS * CHUNK
    hn = _rmsnorm(x, p["norm_mix"], tm=norm_tm, out_dtype=BF16, name=f"{tag}_norm_mix")
    mm = functools.partial(_matmul, [hn], tm=tm, tn=width)
    u = mm([p["w_u"]], out_dtype=BF16, epilogue=_epi_gelu, name=f"{tag}_proj_u")
    vs = mm([p["w_vs"]], out_dtype=F32, epilogue=_epi_gelu_layernorm,
            extras=(("row", p["sgu_ln_g"]), ("row", p["sgu_ln_b"])), name=f"{tag}_proj_vs")
    q = mm([p["w_q"]], out_dtype=F32, head_major=True, name=f"{tag}_proj_q")
    kv = mm([p["w_kv"]], out_dtype=F32, head_major=True, name=f"{tag}_proj_kv")
    gates = mm([p["w_gates"]], out_dtype=act_dtype, epilogue=_epi_sigmoid, name=f"{tag}_proj_gates")
    return u, vs, q, kv, gates


def _channel_stages(x, ya, yb, mem_kv, p, *, tm, norm_tm, tl, peer_tm, per_batch, mem_tm, act_dtype, tag):
    d = x.shape[1]
    x1 = _matmul([ya, yb], [p["w_out_a"], p["w_out_b"]], tm=tm, tn=d // 2, out_dtype=F32,
                 epilogue=_epi_residual, extras=(("full", x),), name=f"{tag}_proj_out")
    hq = _rmsnorm(x1, p["norm_mem"], tm=norm_tm, out_dtype=BF16, name=f"{tag}_norm_mem")
    qm = _matmul([hq], [p["w_mq"]], tm=tm, tn=d // 2, out_dtype=act_dtype, name=f"{tag}_mem_q")
    om = _mem_attn(qm, mem_kv, tm=mem_tm, per_batch=per_batch, out_dtype=act_dtype, name=f"{tag}_mem_attn")
    x2 = _matmul([om], [p["w_mo"]], tm=tm, tn=d // 2, out_dtype=F32, epilogue=_epi_residual,
                 extras=(("full", x1),), name=f"{tag}_mem_o")
    ht = _rmsnorm(x2, p["norm_peer"], tm=tl, out_dtype=BF16, transpose=True, name=f"{tag}_norm_peer")
    qt = _matmul([p["peer_wq_t"]], [ht], tm=d // 2, tn=tl, out_dtype=BF16, name=f"{tag}_peer_q")
    cnt, e1, rank, e2 = _peer_scores(qt, p["peer_keys1"], p["peer_keys2"], tl=tl, name=f"{tag}_peer_scores")
    y = _peer_dense(p["peer_u"], p["peer_v"], ht, cnt, e1, rank, e2, tm=peer_tm, eb=512,
                    name=f"{tag}_peer_dense")
    return _final(x2, y, p["norm_final"], tm=tl, name=f"{tag}_final")


def kernel(x_prompt, x_sample, mem_prompt, cache_win, cache_mem_kv, rel_bias, norm_mix, w_in, sgu_ln_g, sgu_ln_b, sgu_w, sgu_b, w_out, norm_mem, norm_memtok, w_mq, w_mk, w_mv, w_mo, norm_peer, peer_wq, peer_keys1, peer_keys2, peer_u, peer_v, norm_final):
    depth = norm_mix.shape[0]
    batch, seq, d = x_prompt.shape
    dec_batch, dec_seq, _ = x_sample.shape
    assert depth == 1 and batch == 1, "one layer and one prompt sequence are supported"
    width = SGU_GROUPS * CHUNK
    n_past = cache_win.shape[2]
    assert seq % (DIL_PAIRS[-1][0]) == 0 and dec_seq <= CHUNK and CHUNK % dec_seq == 0

    w_in_b = w_in[0].astype(BF16)
    o_q = 2 * width
    o_k = o_q + N_DIL * width
    o_g = o_k + 2 * width
    p = dict(
        norm_mix=norm_mix[0], norm_mem=norm_mem[0], norm_peer=norm_peer[0], norm_final=norm_final,
        w_u=w_in_b[:, :width], w_vs=w_in_b[:, width:o_q], w_q=w_in_b[:, o_q:o_k], w_kv=w_in_b[:, o_k:o_g],
        w_gates=w_in_b[:, o_g:],
        sgu_ln_g=sgu_ln_g[0].reshape(1, width), sgu_ln_b=sgu_ln_b[0].reshape(1, width),
        w_out_a=w_out[0, :width].astype(BF16), w_out_b=w_out[0, width:].astype(BF16),
        w_mq=w_mq[0].astype(BF16), w_mo=w_mo[0].astype(BF16),
        peer_wq_t=peer_wq[0].T.astype(BF16),
        peer_keys1=peer_keys1[0].astype(BF16), peer_keys2=peer_keys2[0].astype(BF16),
        peer_u=peer_u[0].astype(BF16), peer_v=peer_v[0].astype(BF16),
    )

    xp = x_prompt.reshape(seq, d)
    u, vs, q, kv, gates = _input_projection(xp, p, tm=PROMPT_TM, norm_tm=NORM_TM, act_dtype=BF16, tag="p")
    tril = np.tril(np.ones((CHUNK, CHUNK), np.float32))
    bias_full = _spread_bias(sgu_b[0])
    ya = _sgu(u, vs, gates, sgu_w[0], jnp.asarray(tril), bias_full, n_chunks=4, name="p_sgu")
    bias_tabs = jnp.stack([_prompt_bias_table(rel_bias, g, dil) for g, (_, dil) in enumerate(DIL_PAIRS)])
    yb = _dil_attn(q, kv, bias_tabs, gates, name="p_dil_attn")
    mem_n = _rmsnorm(mem_prompt.reshape(-1, d), norm_memtok[0], tm=mem_prompt.shape[1], out_dtype=BF16,
                     name="p_norm_memtok")
    w_mkv = jnp.concatenate([w_mk[0], w_mv[0]], axis=1).astype(BF16)
    mem_kv = _matmul([mem_n], [w_mkv], tm=mem_n.shape[0], tn=d // 2, out_dtype=F32, name="p_mem_kv")
    y_prompt = _channel_stages(xp, ya, yb, mem_kv, p, tm=PROMPT_TM, norm_tm=NORM_TM, tl=PEER_TOKEN_TILE,
                               peer_tm=PEER_TOKEN_TILE, per_batch=False, mem_tm=PEER_TOKEN_TILE,
                               act_dtype=BF16, tag="p")
    win_rows = min(DIL_PAIRS[-1][0], seq)
    state_win_p = kv[:, seq - win_rows:].transpose(1, 0, 2).reshape(1, 1, win_rows, 2, KV_HEADS, HEAD_DIM)
    state_mem_p = mem_kv.reshape(1, 1, mem_kv.shape[0], 2, MEM_HEADS, d // MEM_HEADS)

    ts = dec_batch * dec_seq
    assert ts % CHUNK == 0
    xs = x_sample.reshape(ts, d)
    u, vs, q, kv, gates = _input_projection(xs, p, tm=ts, norm_tm=ts, act_dtype=F32, tag="s")
    reps = CHUNK // dec_seq
    blockdiag = np.kron(np.eye(reps, dtype=np.float32), np.tril(np.ones((dec_seq, dec_seq), np.float32)))
    w_small = jnp.tile(sgu_w[0][:, :dec_seq, :dec_seq], (1, reps, reps))
    bias_small = _spread_bias(jnp.tile(sgu_b[0][:, :dec_seq], (1, reps)))
    ya = _sgu(u, vs, gates, w_small, jnp.asarray(blockdiag), bias_small, n_chunks=1, name="s_sgu")
    n_keys = -(-(n_past + dec_seq) // LANES) * LANES
    tab = _sample_bias_table(rel_bias, n_past, dec_seq, n_keys)
    cache = cache_win[0].reshape(dec_batch, n_past * 2 * KV_HEADS, HEAD_DIM)
    yb = _smp_attn(q, kv, cache, tab, gates, n_new=dec_seq, out_dtype=F32, name="s_smp_attn")
    mem_s = cache_mem_kv[0].reshape(dec_batch, cache_mem_kv.shape[2], 2 * d)
    y_sample = _channel_stages(xs, ya, yb, mem_s, p, tm=ts, norm_tm=ts, tl=ts, peer_tm=ts, per_batch=True,
                               mem_tm=dec_seq, act_dtype=F32, tag="s")
    state_win_s = kv.transpose(1, 0, 2).reshape(1, dec_batch, dec_seq, 2, KV_HEADS, HEAD_DIM)
    state_sgu_s = vs.reshape(1, dec_batch, dec_seq, width)

    return (y_prompt.reshape(batch, seq, d), y_sample.reshape(dec_batch, dec_seq, d),
            state_win_p, state_mem_p, state_win_s, state_sgu_s)
```

```python
import functools
import math

import numpy as np
import jax
import jax.numpy as jnp
from jax import lax
from jax.experimental import pallas as pl
from jax.experimental.pallas import tpu as pltpu

F32 = jnp.float32
BF16 = jnp.bfloat16

EPS = 1e-6
NEG_INF = -1e30

LANES = 128
SUBLANES = 8
VMEM_PHYSICAL_BYTES = 64 * 1024 * 1024
VMEM_COMPILER_RESERVE_BYTES = 6 * 1024 * 1024

CHUNK = 128
SGU_GROUPS = 8
HEAD_DIM = 128
KV_HEADS = 8
DIL_PAIRS = ((128, 1), (512, 4), (2048, 16))
N_DIL = len(DIL_PAIRS)
STEPS = 128
REL_BUCKETS = 32
REL_MAX_DIST = 2048
MEM_HEADS = 4
PEER_HEADS = 8
PEER_N_KEYS = 128
PEER_KEY_DIM = 128
PEER_TOPK = 16

PROMPT_TM = 1024
NORM_TM = 512
PEER_TOKEN_TILE = 512


def _vmem_limit(block_bytes):
    return int(min(block_bytes + VMEM_COMPILER_RESERVE_BYTES, VMEM_PHYSICAL_BYTES - 4 * 1024 * 1024))


def _nbytes(shape, dtype):
    return int(np.prod(shape)) * jnp.dtype(dtype).itemsize


def _params(semantics, block_bytes):
    return pltpu.CompilerParams(dimension_semantics=semantics, vmem_limit_bytes=_vmem_limit(block_bytes))


def _rmsnorm_kernel(x_ref, g_ref, o_ref, *, transpose):
    x = x_ref[...].astype(F32)
    y = x * lax.rsqrt(jnp.mean(x * x, axis=-1, keepdims=True) + EPS) * g_ref[...]
    if transpose:
        y = y.T
    o_ref[...] = y.astype(o_ref.dtype)


def _rmsnorm(x, g, *, tm, out_dtype, transpose=False, name):
    t, d = x.shape
    assert t % tm == 0
    if transpose:
        out_shape = jax.ShapeDtypeStruct((d, t), out_dtype)
        out_spec = pl.BlockSpec((d, tm), lambda i: (0, i))
    else:
        out_shape = jax.ShapeDtypeStruct((t, d), out_dtype)
        out_spec = pl.BlockSpec((tm, d), lambda i: (i, 0))
    blocks = 2 * _nbytes((tm, d), x.dtype) + 2 * _nbytes((tm, d), out_dtype) + 2 * _nbytes((tm, d), F32)
    return pl.pallas_call(
        functools.partial(_rmsnorm_kernel, transpose=transpose),
        grid=(t // tm,),
        in_specs=[pl.BlockSpec((tm, d), lambda i: (i, 0)), pl.BlockSpec((1, d), lambda i: (0, 0))],
        out_specs=out_spec,
        out_shape=out_shape,
        compiler_params=_params(("parallel",), blocks),
        name=name,
    )(x, g.reshape(1, d).astype(F32))


def _epi_cast(acc):
    return acc


def _epi_gelu(acc):
    return jax.nn.gelu(acc)


def _epi_sigmoid(acc):
    return jax.nn.sigmoid(acc)


def _epi_gelu_layernorm(acc, g, b):
    a = jax.nn.gelu(acc)
    mu = jnp.mean(a, axis=-1, keepdims=True)
    var = jnp.mean(jnp.square(a - mu), axis=-1, keepdims=True)
    return (a - mu) * lax.rsqrt(var + EPS) * g + b


def _epi_residual(acc, res):
    return res + acc


def _matmul_kernel(*refs, n_pairs, n_extras, epilogue, head_major):
    out_ref = refs[-1]
    acc = None
    for p in range(n_pairs):
        part = jnp.dot(refs[2 * p][...].astype(BF16), refs[2 * p + 1][...], preferred_element_type=F32)
        acc = part if acc is None else acc + part
    extras = [refs[2 * n_pairs + e][...] for e in range(n_extras)]
    res = epilogue(acc, *extras).astype(out_ref.dtype)
    if head_major:
        for hh in range(out_ref.shape[0]):
            out_ref[hh] = res[:, hh * HEAD_DIM:(hh + 1) * HEAD_DIM]
    else:
        out_ref[...] = res


def _matmul(xs, ws, *, tm, tn, out_dtype, epilogue=_epi_cast, extras=(), head_major=False, name):
    m = xs[0].shape[0]
    n = ws[0].shape[1]
    assert m % tm == 0 and n % tn == 0, (m, tm, n, tn)
    in_specs, args, blocks = [], [], 0
    for x, w in zip(xs, ws):
        k = x.shape[1]
        assert w.shape == (k, n) and x.shape == (m, k)
        in_specs += [pl.BlockSpec((tm, k), lambda i, j: (i, 0)), pl.BlockSpec((k, tn), lambda i, j: (0, j))]
        args += [x, w]
        blocks += 2 * _nbytes((tm, k), x.dtype) + 2 * _nbytes((k, tn), w.dtype)
    for kind, arr in extras:
        if kind == "row":
            assert arr.shape == (1, n)
            in_specs.append(pl.BlockSpec((1, tn), lambda i, j: (0, j)))
        else:
            assert kind == "full" and arr.shape == (m, n)
            in_specs.append(pl.BlockSpec((tm, tn), lambda i, j: (i, j)))
            blocks += 2 * _nbytes((tm, tn), arr.dtype)
        args.append(arr)
    blocks += 2 * _nbytes((tm, tn), out_dtype) + 2 * _nbytes((tm, tn), F32)
    if head_major:
        out_spec = pl.BlockSpec((tn // HEAD_DIM, tm, HEAD_DIM), lambda i, j: (j, i, 0))
        out_shape = jax.ShapeDtypeStruct((n // HEAD_DIM, m, HEAD_DIM), out_dtype)
    else:
        out_spec = pl.BlockSpec((tm, tn), lambda i, j: (i, j))
        out_shape = jax.ShapeDtypeStruct((m, n), out_dtype)
    return pl.pallas_call(
        functools.partial(_matmul_kernel, n_pairs=len(xs), n_extras=len(extras), epilogue=epilogue,
                          head_major=head_major),
        grid=(m // tm, n // tn),
        in_specs=in_specs,
        out_specs=out_spec,
        out_shape=out_shape,
        compiler_params=_params(("parallel", "parallel"), blocks),
        name=name,
    )(*args)


def _sgu_kernel(u_ref, vs_ref, gate_ref, w_ref, mask_ref, b_ref, o_ref, *, n_chunks):
    mask = mask_ref[...]
    for g in range(SGU_GROUPS):
        cols = slice(g * CHUNK, (g + 1) * CHUNK)
        wm = (w_ref[g] * mask).astype(BF16)
        for c in range(n_chunks):
            rows = slice(c * CHUNK, (c + 1) * CHUNK)
            mixed = jnp.dot(wm, vs_ref[rows, cols].astype(BF16), preferred_element_type=F32) + b_ref[:, cols]
            ya = u_ref[rows, cols].astype(F32) * mixed
            o_ref[rows, cols] = (gate_ref[rows, cols].astype(F32) * ya).astype(o_ref.dtype)


def _sgu(u, vs, gates, w_mix, mask, bias_full, *, n_chunks, name):
    t, width = u.shape
    rows = n_chunks * CHUNK
    assert t % rows == 0 and width == SGU_GROUPS * CHUNK
    blocks = (2 * _nbytes((rows, width), u.dtype) + 2 * _nbytes((rows, width), vs.dtype)
              + 2 * _nbytes((rows, width), gates.dtype) + 2 * _nbytes((rows, width), BF16)
              + 2 * _nbytes(w_mix.shape, F32) + 4 * _nbytes((CHUNK, width), F32))
    return pl.pallas_call(
        functools.partial(_sgu_kernel, n_chunks=n_chunks),
        grid=(t // rows,),
        in_specs=[
            pl.BlockSpec((rows, width), lambda i: (i, 0)),
            pl.BlockSpec((rows, width), lambda i: (i, 0)),
            pl.BlockSpec((rows, width), lambda i: (i, 0)),
            pl.BlockSpec((SGU_GROUPS, CHUNK, CHUNK), lambda i: (0, 0, 0)),
            pl.BlockSpec((CHUNK, CHUNK), lambda i: (0, 0)),
            pl.BlockSpec((CHUNK, width), lambda i: (0, 0)),
        ],
        out_specs=pl.BlockSpec((rows, width), lambda i: (i, 0)),
        out_shape=jax.ShapeDtypeStruct((t, width), BF16),
        compiler_params=_params(("parallel",), blocks),
        name=name,
    )(u, vs, gates, w_mix, mask, bias_full)


ATTN_ROWS = DIL_PAIRS[-1][0]
ATTN_UNROLL = 4


def _dil_attn_kernel(q0_ref, q1_ref, q2_ref, kc_ref, vc_ref, kp_ref, vp_ref, bias_ref, gate_ref, o_ref,
                     k_scr, v_scr, og0, og1, og2, lg0, lg1, lg2):
    rows = ATTN_ROWS
    first = pl.program_id(1) == 0
    k_scr[0:rows, :] = kp_ref[...]
    k_scr[rows:2 * rows, :] = kc_ref[...]
    v_scr[0:rows, :] = vp_ref[...]
    v_scr[rows:2 * rows, :] = vc_ref[...]
    prev_cols = (lax.broadcasted_iota(jnp.int32, (1, 2 * STEPS), 1) < STEPS).astype(F32)
    ones = jnp.ones((2 * STEPS, HEAD_DIM), BF16)
    scale = HEAD_DIM ** -0.5
    for g, (q_ref, og, lg) in enumerate(((q0_ref, og0, lg0), (q1_ref, og1, lg1), (q2_ref, og2, lg2))):
        dil = DIL_PAIRS[g][1]
        span = STEPS * dil
        bias = bias_ref[g]

        def block(u, carry, q_ref=q_ref, og=og, lg=lg, dil=dil, span=span, bias=bias):
            sp = u // dil
            base = sp * span + (u - sp * dil)
            q = q_ref[pl.ds(base, STEPS, stride=dil), :].astype(BF16)
            kk = k_scr[pl.ds(rows + base - span, 2 * STEPS, stride=dil), :].astype(BF16)
            vv = v_scr[pl.ds(rows + base - span, 2 * STEPS, stride=dil), :].astype(BF16)
            no_prev = jnp.where(jnp.logical_and(first, sp == 0), NEG_INF, 0.0).astype(F32)
            logits = (lax.dot_general(q, kk, (((1,), (1,)), ((), ())), preferred_element_type=F32) * scale
                      + bias + no_prev * prev_cols)
            m = jnp.max(logits, axis=-1, keepdims=True)
            e = jnp.exp(logits - m).astype(BF16)
            os_ = jnp.dot(e, jnp.concatenate([vv, ones], axis=1), preferred_element_type=F32)
            ssum = os_[:, HEAD_DIM:]
            og[pl.ds(base, STEPS, stride=dil), :] = os_[:, :HEAD_DIM] / ssum
            lg[pl.ds(base, STEPS, stride=dil), :] = m + jnp.log(ssum)
            return carry

        lax.fori_loop(0, rows // STEPS, block, 0, unroll=ATTN_UNROLL)

    def combine(c, carry):
        rs = pl.ds(pl.multiple_of(c * 2 * STEPS, 2 * STEPS), 2 * STEPS)
        l0, l1, l2 = lg0[rs, :], lg1[rs, :], lg2[rs, :]
        mx = jnp.maximum(jnp.maximum(l0, l1), l2)
        w0, w1, w2 = jnp.exp(l0 - mx), jnp.exp(l1 - mx), jnp.exp(l2 - mx)
        mix = (w0 * og0[rs, :] + w1 * og1[rs, :] + w2 * og2[rs, :]) / (w0 + w1 + w2)
        o_ref[rs, :] = (gate_ref[rs, :].astype(F32) * mix).astype(o_ref.dtype)
        return carry

    lax.fori_loop(0, rows // (2 * STEPS), combine, 0)


def _dil_attn(q, kv, bias_tabs, gates, *, name):
    _, s, _ = q.shape
    rows = ATTN_ROWS
    assert s % rows == 0
    slab = lambda off: pl.BlockSpec((None, rows, HEAD_DIM), lambda h, n, off=off: (off + h, n, 0))
    prev = lambda off: pl.BlockSpec((None, rows, HEAD_DIM),
                                    lambda h, n, off=off: (off + h, jnp.maximum(n - 1, 0), 0))
    blocks = (2 * 7 * _nbytes((rows, HEAD_DIM), F32) + 4 * _nbytes((rows, HEAD_DIM), gates.dtype)
              + 2 * _nbytes((N_DIL, STEPS, 2 * STEPS), F32) + 10 * _nbytes((rows, HEAD_DIM), F32))
    return pl.pallas_call(
        _dil_attn_kernel,
        grid=(KV_HEADS, s // rows),
        in_specs=[
            slab(0), slab(KV_HEADS), slab(2 * KV_HEADS),
            slab(0), slab(KV_HEADS),
            prev(0), prev(KV_HEADS),
            pl.BlockSpec((N_DIL, None, STEPS, 2 * STEPS), lambda h, n: (0, h, 0, 0)),
            pl.BlockSpec((rows, HEAD_DIM), lambda h, n: (n, KV_HEADS + h)),
        ],
        out_specs=pl.BlockSpec((rows, HEAD_DIM), lambda h, n: (n, h)),
        out_shape=jax.ShapeDtypeStruct((s, KV_HEADS * HEAD_DIM), BF16),
        scratch_shapes=[pltpu.VMEM((2 * rows, HEAD_DIM), F32), pltpu.VMEM((2 * rows, HEAD_DIM), F32)]
        + [pltpu.VMEM((rows, HEAD_DIM), F32)] * 6,
        compiler_params=_params(("parallel", "parallel"), blocks),
        name=name,
    )(q, q, q, kv, kv, kv, kv, bias_tabs, gates)


def _smp_attn_kernel(q_ref, kvn_ref, cache_ref, tab_ref, gate_ref, o_ref, *, n_past, n_new, n_keys):
    pad = jnp.zeros((n_keys - n_past - n_new, HEAD_DIM), F32)
    rows_per_pos = 2 * KV_HEADS

    def head(h, carry):
        keys = jnp.concatenate([cache_ref[pl.ds(h, n_past, stride=rows_per_pos), :], kvn_ref[h], pad], axis=0)
        vals = jnp.concatenate([cache_ref[pl.ds(KV_HEADS + h, n_past, stride=rows_per_pos), :],
                                kvn_ref[KV_HEADS + h], pad], axis=0)
        q = jnp.concatenate([q_ref[h], q_ref[KV_HEADS + h], q_ref[2 * KV_HEADS + h],
                             jnp.zeros((n_new, HEAD_DIM), F32)], axis=0)
        logits = lax.dot_general(q.astype(BF16), keys.astype(BF16), (((1,), (1,)), ((), ())),
                                 preferred_element_type=F32) * (HEAD_DIM ** -0.5) + tab_ref[h]
        m = jnp.max(logits, axis=-1, keepdims=True)
        e = jnp.exp(logits - m)
        s = jnp.sum(e, axis=-1, keepdims=True)
        o = jnp.dot(e.astype(BF16), vals.astype(BF16), preferred_element_type=F32) / s
        lse = m + jnp.log(s)
        ls = [lse[g * n_new:(g + 1) * n_new] for g in range(N_DIL)]
        mx = jnp.maximum(jnp.maximum(ls[0], ls[1]), ls[2])
        ws = [jnp.exp(l - mx) for l in ls]
        mix = sum(ws[g] * o[g * n_new:(g + 1) * n_new] for g in range(N_DIL)) / (ws[0] + ws[1] + ws[2])
        cols = pl.ds(pl.multiple_of(h * HEAD_DIM, HEAD_DIM), HEAD_DIM)
        o_ref[:, cols] = (gate_ref[:, cols].astype(F32) * mix).astype(o_ref.dtype)
        return carry

    lax.fori_loop(0, KV_HEADS, head, 0)


def _smp_attn(q, kv, cache, tab, gates, *, n_new, out_dtype, name):
    b, cache_rows, _ = cache.shape
    n_past = cache_rows // (2 * KV_HEADS)
    n_keys = tab.shape[-1]
    w = KV_HEADS * HEAD_DIM
    blocks = (2 * _nbytes((cache_rows, HEAD_DIM), F32) + 2 * _nbytes((n_keys, HEAD_DIM), F32)
              + 2 * _nbytes(tab.shape, F32) + 8 * _nbytes(((N_DIL + 1) * n_new, n_keys), F32))
    return pl.pallas_call(
        functools.partial(_smp_attn_kernel, n_past=n_past, n_new=n_new, n_keys=n_keys),
        grid=(b,),
        in_specs=[
            pl.BlockSpec((N_DIL * KV_HEADS, n_new, HEAD_DIM), lambda i: (0, i, 0)),
            pl.BlockSpec((2 * KV_HEADS, n_new, HEAD_DIM), lambda i: (0, i, 0)),
            pl.BlockSpec((None, cache_rows, HEAD_DIM), lambda i: (i, 0, 0)),
            pl.BlockSpec(tab.shape, lambda i: (0, 0, 0)),
            pl.BlockSpec((n_new, w), lambda i: (i, 1)),
        ],
        out_specs=pl.BlockSpec((n_new, w), lambda i: (i, 0)),
        out_shape=jax.ShapeDtypeStruct((b * n_new, w), out_dtype),
        compiler_params=_params(("parallel",), blocks),
        name=name,
    )(q, kv, cache, tab, gates)


def _mem_attn_kernel(q_ref, k_ref, v_ref, o_ref, *, head_dim):
    scale = head_dim ** -0.5
    for h in range(q_ref.shape[1] // head_dim):
        cs = slice(h * head_dim, (h + 1) * head_dim)
        logits = lax.dot_general(q_ref[:, cs].astype(BF16), k_ref[:, cs].astype(BF16), (((1,), (1,)), ((), ())),
                                 preferred_element_type=F32) * scale
        m = jnp.max(logits, axis=-1, keepdims=True)
        e = jnp.exp(logits - m)
        s = jnp.sum(e, axis=-1, keepdims=True)
        o = jnp.dot(e.astype(BF16), v_ref[:, cs].astype(BF16), preferred_element_type=F32)
        o_ref[:, cs] = (o / s).astype(o_ref.dtype)


def _mem_attn(q, mem_kv, *, tm, per_batch, out_dtype, name):
    t, w = q.shape
    m_tok = mem_kv.shape[-2]
    if per_batch:
        kspec = pl.BlockSpec((None, m_tok, w), lambda i: (i, 0, 0))
        vspec = pl.BlockSpec((None, m_tok, w), lambda i: (i, 0, 1))
    else:
        kspec = pl.BlockSpec((m_tok, w), lambda i: (0, 0))
        vspec = pl.BlockSpec((m_tok, w), lambda i: (0, 1))
    blocks = 4 * _nbytes((tm, w), BF16) + 4 * _nbytes((m_tok, w), F32) + 6 * _nbytes((tm, m_tok), F32) \
        + 2 * _nbytes((tm, w), F32)
    return pl.pallas_call(
        functools.partial(_mem_attn_kernel, head_dim=w // MEM_HEADS),
        grid=(t // tm,),
        in_specs=[pl.BlockSpec((tm, w), lambda i: (i, 0)), kspec, vspec],
        out_specs=pl.BlockSpec((tm, w), lambda i: (i, 0)),
        out_shape=jax.ShapeDtypeStruct((t, w), out_dtype),
        compiler_params=_params(("parallel",), blocks),
        name=name,
    )(q, mem_kv, mem_kv)


def _oddeven_merge_sort_pairs(n):
    pairs = []

    def merge(lo, hi, r):
        step = r * 2
        if step < hi - lo:
            merge(lo, hi, step)
            merge(lo + r, hi, step)
            pairs.extend((i, i + r) for i in range(lo + r, hi - r, step))
        else:
            pairs.append((lo, lo + r))

    def sort(lo, hi):
        if hi - lo >= 1:
            mid = lo + (hi - lo) // 2
            sort(lo, mid)
            sort(mid + 1, hi)
            merge(lo, hi, 1)

    sort(0, n - 1)
    return pairs


def _compare_exchange(wires, i, j):
    a, b = wires[i], wires[j]
    if b is None:
        return
    if a is None:
        wires[i], wires[j] = b, None
        return
    wires[i], wires[j] = jnp.maximum(a, b), jnp.minimum(a, b)


def _sort_desc(wires):
    for i, j in _oddeven_merge_sort_pairs(len(wires)):
        _compare_exchange(wires, i, j)
    return wires


def _bitonic_merge_desc(wires):
    n = len(wires)
    dist = n // 2
    while dist >= 1:
        for i in range(n):
            if i & dist == 0:
                _compare_exchange(wires, i, i + dist)
        dist //= 2
    return wires


def _top16_desc(s):
    wires = [s[SUBLANES * i:SUBLANES * (i + 1), :] for i in range(PEER_N_KEYS // SUBLANES)]
    wires = _sort_desc(wires)
    for shift in (4, 2, 1):
        wires = [jnp.maximum(wires[i], pltpu.roll(wires[PEER_TOPK - 1 - i], shift, 0)) for i in range(PEER_TOPK)]
        wires = _bitonic_merge_desc(wires)
    return wires


_CAND_PAIRS = [(j, k) for j in range(PEER_TOPK) for k in range(PEER_TOPK) if (j + 1) * (k + 1) <= PEER_TOPK]


def _peer_scores_kernel(qt_ref, k1_ref, k2_ref, cnt_ref, e1_ref, rank_ref, e2_ref, s1_ref, s2_ref,
                        *, n_lane_tiles):
    for h in range(PEER_HEADS):
        r0 = h * 2 * PEER_KEY_DIM
        s1_ref[h] = jnp.dot(k1_ref[h], qt_ref[r0:r0 + PEER_KEY_DIM, :], preferred_element_type=F32)
        s2_ref[h] = jnp.dot(k2_ref[h], qt_ref[r0 + PEER_KEY_DIM:r0 + 2 * PEER_KEY_DIM, :],
                            preferred_element_type=F32)
    sub = lax.broadcasted_iota(jnp.int32, (SUBLANES, LANES), 0)

    def lane_tile(lt, carry):
        lanes = pl.ds(pl.multiple_of(lt * LANES, LANES), LANES)
        a_pack = [None] * PEER_TOPK
        b_pack = [None] * PEER_TOPK
        for h in range(PEER_HEADS):
            ta = _top16_desc(s1_ref[h, :, lanes])
            tb = _top16_desc(s2_ref[h, :, lanes])
            for j in range(PEER_TOPK):
                a_pack[j] = ta[j] if h == 0 else jnp.where(sub == h, ta[j], a_pack[j])
                b_pack[j] = tb[j] if h == 0 else jnp.where(sub == h, tb[j], b_pack[j])
        cands = [a_pack[j] + b_pack[k] for j, k in _CAND_PAIRS]
        top = cands[0]
        wires = _sort_desc(list(cands) + [None] * (64 - len(cands)))
        tau = wires[PEER_TOPK - 1]
        z = jnp.zeros_like(tau)
        for c in cands:
            z = z + jnp.where(c >= tau, jnp.exp(c - top), 0.0)
        zinv = 1.0 / z
        for h in range(PEER_HEADS):
            s1 = s1_ref[h, :, lanes]
            s2 = s2_ref[h, :, lanes]
            tau_h = tau[h:h + 1, :]
            cnt = jnp.zeros_like(s1)
            rank = jnp.zeros_like(s2)
            for k in range(PEER_TOPK):
                b_k = b_pack[k][h:h + 1, :]
                cnt = jnp.where((s1 + b_k) >= tau_h, k + 1.0, cnt)
                rank = jnp.where(b_k > s2, k + 1.0, rank)
            cnt_ref[h, :, lanes] = cnt
            rank_ref[h, :, lanes] = rank.astype(rank_ref.dtype)
            e1_ref[h, :, lanes] = jnp.exp(s1 - a_pack[0][h:h + 1, :])
            e2_ref[h, :, lanes] = (jnp.exp(s2 - b_pack[0][h:h + 1, :]) * zinv[h:h + 1, :]).astype(e2_ref.dtype)
        return carry

    lax.fori_loop(0, n_lane_tiles, lane_tile, 0)


def _peer_scores(qt, keys1, keys2, *, tl, name):
    d, t = qt.shape
    assert t % tl == 0 and d == PEER_HEADS * 2 * PEER_KEY_DIM
    first = jax.ShapeDtypeStruct((PEER_HEADS, PEER_N_KEYS, t), F32)
    second = jax.ShapeDtypeStruct((PEER_HEADS, PEER_N_KEYS, t), BF16)
    bspec = lambda: pl.BlockSpec((PEER_HEADS, PEER_N_KEYS, tl), lambda i: (0, 0, i))
    blocks = 2 * _nbytes((d, tl), BF16) + 4 * _nbytes(keys1.shape, BF16) + 8 * _nbytes((PEER_HEADS, PEER_N_KEYS, tl), F32)
    return pl.pallas_call(
        functools.partial(_peer_scores_kernel, n_lane_tiles=tl // LANES),
        grid=(t // tl,),
        in_specs=[
            pl.BlockSpec((d, tl), lambda i: (0, i)),
            pl.BlockSpec(keys1.shape, lambda i: (0, 0, 0)),
            pl.BlockSpec(keys2.shape, lambda i: (0, 0, 0)),
        ],
        out_specs=[bspec(), bspec(), bspec(), bspec()],
        out_shape=[first, first, second, second],
        scratch_shapes=[pltpu.VMEM((PEER_HEADS, PEER_N_KEYS, tl), F32)] * 2,
        compiler_params=_params(("parallel",), blocks),
        name=name,
    )(qt, keys1, keys2)


def _peer_dense_step(u_ref, v_ref, ht_ref, cnt_ref, e1_ref, rank_ref, e2_ref, o_ref,
                     a_new, a_old, w_new, w_old, key_row_in_group, rows_per_block, n_lane_tiles,
                     activate=True, gate=True, accumulate=True):
    half = PEER_N_KEYS // 2
    d = o_ref.shape[1]
    bf16_rows = 2 * SUBLANES

    def gate_lane_tile(lt):
        lanes = slice(lt * LANES, (lt + 1) * LANES)
        for i0 in range(0, rows_per_block, 2):
            gates = {}
            for h in range(PEER_HEADS):
                cnt_rows = cnt_ref[h, :, lanes]
                e1_rows = e1_ref[h, :, lanes]
                rank = rank_ref[h, :, lanes]
                e2t = e2_ref[h, :, lanes]
                for ii in (i0, i0 + 1):
                    r = key_row_in_group + ii
                    cnt_b = jnp.broadcast_to(cnt_rows[r:r + 1, :], (PEER_N_KEYS, LANES)).astype(BF16)
                    e1_b = jnp.broadcast_to(e1_rows[r:r + 1, :], (PEER_N_KEYS, LANES)).astype(BF16)
                    hit = jnp.minimum(jnp.maximum(cnt_b - rank, 0), 1)
                    term = (hit * e1_b) * e2t
                    gates[ii] = term if h == 0 else gates[ii] + term
            for ii in (i0, i0 + 1):
                erows = slice(ii * PEER_N_KEYS, (ii + 1) * PEER_N_KEYS)
                tile = gates[ii].astype(F32) * jax.nn.gelu(a_old[erows, lanes])
                w_new[lanes, erows] = tile.T.astype(w_new.dtype)

    n_slices = n_lane_tiles // 2
    for c in range(n_slices):
        lanes2 = slice(2 * c * LANES, 2 * (c + 1) * LANES)
        cols = slice(c * d // n_slices, (c + 1) * d // n_slices)
        if activate:
            a_new[:, lanes2] = jnp.dot(u_ref[...], ht_ref[:, lanes2], preferred_element_type=F32)
        if gate:
            gate_lane_tile(2 * c)
        if accumulate:
            o_ref[:, cols] += jnp.dot(w_old[...], v_ref[:, cols], preferred_element_type=F32)
        if gate:
            gate_lane_tile(2 * c + 1)


def _peer_dense_kernel(u_ref, v_ref, ht_ref, cnt_ref, e1_ref, rank_ref, e2_ref, o_ref,
                       a0, a1, w0, w1, *, rows_per_block, n_lane_tiles, n_blocks):
    j = pl.program_id(1)

    @pl.when(j == 0)
    def _():
        o_ref[...] = jnp.zeros_like(o_ref)

    blocks_per_group = SUBLANES // rows_per_block
    step = functools.partial(_peer_dense_step, u_ref, v_ref, ht_ref, cnt_ref, e1_ref, rank_ref, e2_ref,
                             o_ref, rows_per_block=rows_per_block, n_lane_tiles=n_lane_tiles)
    even_step = functools.partial(step, a_new=a0, a_old=a1, w_new=w1, w_old=w0,
                                  key_row_in_group=(blocks_per_group - 1) * rows_per_block)
    odd_step = functools.partial(step, a_new=a1, a_old=a0, w_new=w0, w_old=w1, key_row_in_group=0)
    even = lax.rem(j, 2) == 0
    filling = j < 2
    draining = j >= n_blocks
    steady = jnp.logical_not(jnp.logical_or(filling, draining))

    pl.when(jnp.logical_and(even, steady))(even_step)
    pl.when(jnp.logical_and(jnp.logical_not(even), steady))(odd_step)
    pl.when(j == 0)(functools.partial(even_step, gate=False, accumulate=False))
    pl.when(j == 1)(functools.partial(odd_step, accumulate=False))
    pl.when(j == n_blocks)(functools.partial(even_step, activate=False))
    pl.when(j == n_blocks + 1)(functools.partial(odd_step, activate=False, gate=False))


def _peer_dense(u_tab, v_tab, ht, cnt, e1, rank, e2, *, tm, eb, name):
    n_exp, d = u_tab.shape
    t = ht.shape[1]
    assert t % tm == 0 and tm % (2 * LANES) == 0 and n_exp % eb == 0 and eb % PEER_N_KEYS == 0
    assert eb // PEER_N_KEYS in (SUBLANES // 2, SUBLANES)
    n_blocks = n_exp // eb
    assert n_blocks % 2 == 0
    last = n_blocks - 1
    full = lambda: pl.BlockSpec((PEER_HEADS, PEER_N_KEYS, tm), lambda i, j: (0, 0, i))
    blocks_per_group = SUBLANES * PEER_N_KEYS // eb
    group = lambda: pl.BlockSpec((PEER_HEADS, SUBLANES, tm),
                                 lambda i, j: (0, jnp.clip(j - 1, 0, last) // blocks_per_group, i))
    blocks = (2 * _nbytes((eb, d), BF16) * 2 + 2 * _nbytes((d, tm), BF16)
              + 4 * _nbytes((PEER_HEADS, PEER_N_KEYS, tm), BF16) + 4 * _nbytes((PEER_HEADS, SUBLANES, tm), F32)
              + 2 * _nbytes((d, tm), F32) + 2 * _nbytes((eb, tm), F32) + 2 * _nbytes((eb, tm), BF16)
              + 2 * _nbytes((d, tm), F32))
    return pl.pallas_call(
        functools.partial(_peer_dense_kernel, rows_per_block=eb // PEER_N_KEYS, n_lane_tiles=tm // LANES,
                          n_blocks=n_blocks),
        grid=(t // tm, n_blocks + 2),
        in_specs=[
            pl.BlockSpec((eb, d), lambda i, j: (jnp.minimum(j, last), 0)),
            pl.BlockSpec((eb, d), lambda i, j: (jnp.clip(j - 2, 0, last), 0)),
            pl.BlockSpec((d, tm), lambda i, j: (0, i)),
            group(), group(), full(), full(),
        ],
        out_specs=pl.BlockSpec((tm, d), lambda i, j: (i, 0)),
        out_shape=jax.ShapeDtypeStruct((t, d), F32),
        scratch_shapes=[pltpu.VMEM((eb, tm), F32), pltpu.VMEM((eb, tm), F32),
                        pltpu.VMEM((tm, eb), BF16), pltpu.VMEM((tm, eb), BF16)],
        compiler_params=_params(("parallel", "arbitrary"), blocks),
        name=name,
    )(u_tab, v_tab, ht, cnt, e1, rank, e2)


def _final_kernel(x_ref, y_ref, g_ref, o_ref):
    x = x_ref[...] + y_ref[...]
    o_ref[...] = (x * lax.rsqrt(jnp.mean(x * x, axis=-1, keepdims=True) + EPS) * g_ref[...]).astype(o_ref.dtype)


def _final(x, y, g, *, tm, name):
    t, d = x.shape
    blocks = 6 * _nbytes((tm, d), F32) + 2 * _nbytes((tm, d), F32)
    row = lambda: pl.BlockSpec((tm, d), lambda i: (i, 0))
    return pl.pallas_call(
        _final_kernel,
        grid=(t // tm,),
        in_specs=[row(), row(), pl.BlockSpec((1, d), lambda i: (0, 0))],
        out_specs=row(),
        out_shape=jax.ShapeDtypeStruct((t, d), F32),
        compiler_params=_params(("parallel",), blocks),
        name=name,
    )(x, y, g.reshape(1, d).astype(F32))


def _t5_bucket(dist):
    max_exact = REL_BUCKETS // 2
    d = np.maximum(dist, 1).astype(np.float64)
    large = max_exact + (np.log(d / max_exact) / math.log(REL_MAX_DIST / max_exact)
                         * (REL_BUCKETS - max_exact)).astype(np.int64)
    large = np.minimum(large, REL_BUCKETS - 1)
    return np.where(dist < max_exact, dist, large).astype(np.int32)


def _spread_bias(b):
    return jnp.broadcast_to(b.T[:, :, None], (CHUNK, SGU_GROUPS, CHUNK)).reshape(CHUNK, SGU_GROUPS * CHUNK)


def _group_bias(rel_bias, group, dil):
    bucket = _t5_bucket(dil * np.arange(STEPS + 1))
    table = rel_bias[:, group * KV_HEADS:(group + 1) * KV_HEADS].astype(F32)
    runs, start = [], 0
    for i in range(1, len(bucket) + 1):
        if i == len(bucket) or bucket[i] != bucket[start]:
            runs.append(jnp.broadcast_to(table[bucket[start]:bucket[start] + 1], (i - start, KV_HEADS)))
            start = i
    return jnp.concatenate(runs, axis=0).T


def _prompt_bias_table(rel_bias, group, dil):
    period = 3 * STEPS
    bias = _group_bias(rel_bias, group, dil)
    vec = jnp.concatenate([bias[:, ::-1], jnp.full((KV_HEADS, period - STEPS - 1), NEG_INF, F32)], axis=1)
    flat = jnp.broadcast_to(vec[:, None, :], (KV_HEADS, STEPS, period)).reshape(KV_HEADS, STEPS * period)
    skew = flat[:, :STEPS * (period - 1)].reshape(KV_HEADS, STEPS, period - 1)
    return skew[:, :, :2 * STEPS]


def _sample_bias_table(rel_bias, n_past, n_new, n_keys):
    rows = []
    for g, (win, dil) in enumerate(DIL_PAIRS):
        rev = _group_bias(rel_bias, g, dil)[:, ::-1]
        for t in range(n_new):
            lo = n_past + t - win
            hi = n_keys - (n_past + t + 1)
            rows.append(lax.pad(rev, jnp.asarray(NEG_INF, F32), [(0, 0, 0), (lo, hi, dil - 1)]))
    rows += [jnp.zeros((KV_HEADS, n_keys), F32)] * n_new
    return jnp.stack(rows, axis=1)


def _input_projection(x, p, *, tm, norm_tm, act_dtype, tag):
    width = SGU_GROUPS * CHUNK
    hn = _rmsnorm(x, p["norm_mix"], tm=norm_tm, out_dtype=BF16, name=f"{tag}_norm_mix")
    mm = functools.partial(_matmul, [hn], tm=tm, tn=width)
    u = mm([p["w_u"]], out_dtype=BF16, epilogue=_epi_gelu, name=f"{tag}_proj_u")
    vs = mm([p["w_vs"]], out_dtype=F32, epilogue=_epi_gelu_layernorm,
            extras=(("row", p["sgu_ln_g"]), ("row", p["sgu_ln_b"])), name=f"{tag}_proj_vs")
    q = mm([p["w_q"]], out_dtype=F32, head_major=True, name=f"{tag}_proj_q")
    kv = mm([p["w_kv"]], out_dtype=F32, head_major=True, name=f"{tag}_proj_kv")
    gates = mm([p["w_gates"]], out_dtype=act_dtype, epilogue=_epi_sigmoid, name=f"{tag}_proj_gates")
    return u, vs, q, kv, gates


def _channel_stages(x, ya, yb, mem_kv, p, *, tm, norm_tm, tl, peer_tm, per_batch, mem_tm, act_dtype, tag):
    d = x.shape[1]
    x1 = _matmul([ya, yb], [p["w_out_a"], p["w_out_b"]], tm=tm, tn=d // 2, out_dtype=F32,
                 epilogue=_epi_residual, extras=(("full", x),), name=f"{tag}_proj_out")
    hq = _rmsnorm(x1, p["norm_mem"], tm=norm_tm, out_dtype=BF16, name=f"{tag}_norm_mem")
    qm = _matmul([hq], [p["w_mq"]], tm=tm, tn=d // 2, out_dtype=act_dtype, name=f"{tag}_mem_q")
    om = _mem_attn(qm, mem_kv, tm=mem_tm, per_batch=per_batch, out_dtype=act_dtype, name=f"{tag}_mem_attn")
    x2 = _matmul([om], [p["w_mo"]], tm=tm, tn=d // 2, out_dtype=F32, epilogue=_epi_residual,
                 extras=(("full", x1),), name=f"{tag}_mem_o")
    ht = _rmsnorm(x2, p["norm_peer"], tm=tl, out_dtype=BF16, transpose=True, name=f"{tag}_norm_peer")
    qt = _matmul([p["peer_wq_t"]], [ht], tm=d // 2, tn=tl, out_dtype=BF16, name=f"{tag}_peer_q")
    cnt, e1, rank, e2 = _peer_scores(qt, p["peer_keys1"], p["peer_keys2"], tl=tl, name=f"{tag}_peer_scores")
    y = _peer_dense(p["peer_u"], p["peer_v"], ht, cnt, e1, rank, e2, tm=peer_tm, eb=512,
                    name=f"{tag}_peer_dense")
    return _final(x2, y, p["norm_final"], tm=tl, name=f"{tag}_final")


def kernel(x_prompt, x_sample, mem_prompt, cache_win, cache_mem_kv, rel_bias, norm_mix, w_in, sgu_ln_g, sgu_ln_b, sgu_w, sgu_b, w_out, norm_mem, norm_memtok, w_mq, w_mk, w_mv, w_mo, norm_peer, peer_wq, peer_keys1, peer_keys2, peer_u, peer_v, norm_final):
    depth = norm_mix.shape[0]
    batch, seq, d = x_prompt.shape
    dec_batch, dec_seq, _ = x_sample.shape
    assert depth == 1 and batch == 1, "one layer and one prompt sequence are supported"
    width = SGU_GROUPS * CHUNK
    n_past = cache_win.shape[2]
    assert seq % (DIL_PAIRS[-1][0]) == 0 and dec_seq <= CHUNK and CHUNK % dec_seq == 0

    w_in_b = w_in[0].astype(BF16)
    o_q = 2 * width
    o_k = o_q + N_DIL * width
    o_g = o_k + 2 * width
    p = dict(
        norm_mix=norm_mix[0], norm_mem=norm_mem[0], norm_peer=norm_peer[0], norm_final=norm_final,
        w_u=w_in_b[:, :width], w_vs=w_in_b[:, width:o_q], w_q=w_in_b[:, o_q:o_k], w_kv=w_in_b[:, o_k:o_g],
        w_gates=w_in_b[:, o_g:],
        sgu_ln_g=sgu_ln_g[0].reshape(1, width), sgu_ln_b=sgu_ln_b[0].reshape(1, width),
        w_out_a=w_out[0, :width].astype(BF16), w_out_b=w_out[0, width:].astype(BF16),
        w_mq=w_mq[0].astype(BF16), w_mo=w_mo[0].astype(BF16),
        peer_wq_t=peer_wq[0].T.astype(BF16),
        peer_keys1=peer_keys1[0].astype(BF16), peer_keys2=peer_keys2[0].astype(BF16),
        peer_u=peer_u[0].astype(BF16), peer_v=peer_v[0].astype(BF16),
    )

    xp = x_prompt.reshape(seq, d)
    u, vs, q, kv, gates = _input_projection(xp, p, tm=PROMPT_TM, norm_tm=NORM_TM, act_dtype=BF16, tag="p")
    tril = np.tril(np.ones((CHUNK, CHUNK), np.float32))
    bias_full = _spread_bias(sgu_b[0])
    ya = _sgu(u, vs, gates, sgu_w[0], jnp.asarray(tril), bias_full, n_chunks=4, name="p_sgu")
    bias_tabs = jnp.stack([_prompt_bias_table(rel_bias, g, dil) for g, (_, dil) in enumerate(DIL_PAIRS)])
    yb = _dil_attn(q, kv, bias_tabs, gates, name="p_dil_attn")
    mem_n = _rmsnorm(mem_prompt.reshape(-1, d), norm_memtok[0], tm=mem_prompt.shape[1], out_dtype=BF16,
                     name="p_norm_memtok")
    w_mkv = jnp.concatenate([w_mk[0], w_mv[0]], axis=1).astype(BF16)
    mem_kv = _matmul([mem_n], [w_mkv], tm=mem_n.shape[0], tn=d // 2, out_dtype=F32, name="p_mem_kv")
    y_prompt = _channel_stages(xp, ya, yb, mem_kv, p, tm=PROMPT_TM, norm_tm=NORM_TM, tl=PEER_TOKEN_TILE,
                               peer_tm=PEER_TOKEN_TILE, per_batch=False, mem_tm=PEER_TOKEN_TILE,
                               act_dtype=BF16, tag="p")
    win_rows = min(DIL_PAIRS[-1][0], seq)
    state_win_p = kv[:, seq - win_rows:].transpose(1, 0, 2).reshape(1, 1, win_rows, 2, KV_HEADS, HEAD_DIM)
    state_mem_p = mem_kv.reshape(1, 1, mem_kv.shape[0], 2, MEM_HEADS, d // MEM_HEADS)

    ts = dec_batch * dec_seq
    assert ts % CHUNK == 0
    xs = x_sample.reshape(ts, d)
    u, vs, q, kv, gates = _input_projection(xs, p, tm=ts, norm_tm=ts, act_dtype=F32, tag="s")
    reps = CHUNK // dec_seq
    blockdiag = np.kron(np.eye(reps, dtype=np.float32), np.tril(np.ones((dec_seq, dec_seq), np.float32)))
    w_small = jnp.tile(sgu_w[0][:, :dec_seq, :dec_seq], (1, reps, reps))
    bias_small = _spread_bias(jnp.tile(sgu_b[0][:, :dec_seq], (1, reps)))
    ya = _sgu(u, vs, gates, w_small, jnp.asarray(blockdiag), bias_small, n_chunks=1, name="s_sgu")
    n_keys = -(-(n_past + dec_seq) // LANES) * LANES
    tab = _sample_bias_table(rel_bias, n_past, dec_seq, n_keys)
    cache = cache_win[0].reshape(dec_batch, n_past * 2 * KV_HEADS, HEAD_DIM)
    yb = _smp_attn(q, kv, cache, tab, gates, n_new=dec_seq, out_dtype=F32, name="s_smp_attn")
    mem_s = cache_mem_kv[0].reshape(dec_batch, cache_mem_kv.shape[2], 2 * d)
    y_sample = _channel_stages(xs, ya, yb, mem_s, p, tm=ts, norm_tm=ts, tl=ts, peer_tm=ts, per_batch=True,
                               mem_tm=dec_seq, act_dtype=F32, tag="s")
    state_win_s = kv.transpose(1, 0, 2).reshape(1, dec_batch, dec_seq, 2, KV_HEADS, HEAD_DIM)
    state_sgu_s = vs.reshape(1, dec_batch, dec_seq, width)

    return (y_prompt.reshape(batch, seq, d), y_sample.reshape(dec_batch, dec_seq, d),
            state_win_p, state_mem_p, state_win_s, state_sgu_s)
```

```python
import functools
import math

import numpy as np
import jax
import jax.numpy as jnp
from jax import lax
from jax.experimental import pallas as pl
from jax.experimental.pallas import tpu as pltpu

F32 = jnp.float32
BF16 = jnp.bfloat16

EPS = 1e-6
NEG_INF = -1e30

LANES = 128
SUBLANES = 8
VMEM_PHYSICAL_BYTES = 64 * 1024 * 1024
VMEM_COMPILER_RESERVE_BYTES = 6 * 1024 * 1024

CHUNK = 128
SGU_GROUPS = 8
HEAD_DIM = 128
KV_HEADS = 8
DIL_PAIRS = ((128, 1), (512, 4), (2048, 16))
N_DIL = len(DIL_PAIRS)
STEPS = 128
REL_BUCKETS = 32
REL_MAX_DIST = 2048
MEM_HEADS = 4
PEER_HEADS = 8
PEER_N_KEYS = 128
PEER_KEY_DIM = 128
PEER_TOPK = 16

PROMPT_TM = 1024
NORM_TM = 512
PEER_TOKEN_TILE = 512


def _vmem_limit(block_bytes):
    return int(min(block_bytes + VMEM_COMPILER_RESERVE_BYTES, VMEM_PHYSICAL_BYTES - 4 * 1024 * 1024))


def _nbytes(shape, dtype):
    return int(np.prod(shape)) * jnp.dtype(dtype).itemsize


def _params(semantics, block_bytes):
    return pltpu.CompilerParams(dimension_semantics=semantics, vmem_limit_bytes=_vmem_limit(block_bytes))


def _rmsnorm_kernel(x_ref, g_ref, o_ref, *, transpose):
    x = x_ref[...].astype(F32)
    y = x * lax.rsqrt(jnp.mean(x * x, axis=-1, keepdims=True) + EPS) * g_ref[...]
    if transpose:
        y = y.T
    o_ref[...] = y.astype(o_ref.dtype)


def _rmsnorm(x, g, *, tm, out_dtype, transpose=False, name):
    t, d = x.shape
    assert t % tm == 0
    if transpose:
        out_shape = jax.ShapeDtypeStruct((d, t), out_dtype)
        out_spec = pl.BlockSpec((d, tm), lambda i: (0, i))
    else:
        out_shape = jax.ShapeDtypeStruct((t, d), out_dtype)
        out_spec = pl.BlockSpec((tm, d), lambda i: (i, 0))
    blocks = 2 * _nbytes((tm, d), x.dtype) + 2 * _nbytes((tm, d), out_dtype) + 2 * _nbytes((tm, d), F32)
    return pl.pallas_call(
        functools.partial(_rmsnorm_kernel, transpose=transpose),
        grid=(t // tm,),
        in_specs=[pl.BlockSpec((tm, d), lambda i: (i, 0)), pl.BlockSpec((1, d), lambda i: (0, 0))],
        out_specs=out_spec,
        out_shape=out_shape,
        compiler_params=_params(("parallel",), blocks),
        name=name,
    )(x, g.reshape(1, d).astype(F32))


def _epi_cast(acc):
    return acc


def _epi_gelu(acc):
    return jax.nn.gelu(acc)


def _epi_sigmoid(acc):
    return jax.nn.sigmoid(acc)


def _epi_gelu_layernorm(acc, g, b):
    a = jax.nn.gelu(acc)
    mu = jnp.mean(a, axis=-1, keepdims=True)
    var = jnp.mean(jnp.square(a - mu), axis=-1, keepdims=True)
    return (a - mu) * lax.rsqrt(var + EPS) * g + b


def _epi_residual(acc, res):
    return res + acc


def _matmul_kernel(*refs, n_pairs, n_extras, epilogue, head_major):
    out_ref = refs[-1]
    acc = None
    for p in range(n_pairs):
        part = jnp.dot(refs[2 * p][...].astype(BF16), refs[2 * p + 1][...], preferred_element_type=F32)
        acc = part if acc is None else acc + part
    extras = [refs[2 * n_pairs + e][...] for e in range(n_extras)]
    res = epilogue(acc, *extras).astype(out_ref.dtype)
    if head_major:
        for hh in range(out_ref.shape[0]):
            out_ref[hh] = res[:, hh * HEAD_DIM:(hh + 1) * HEAD_DIM]
    else:
        out_ref[...] = res


def _matmul(xs, ws, *, tm, tn, out_dtype, epilogue=_epi_cast, extras=(), head_major=False, name):
    m = xs[0].shape[0]
    n = ws[0].shape[1]
    assert m % tm == 0 and n % tn == 0, (m, tm, n, tn)
    in_specs, args, blocks = [], [], 0
    for x, w in zip(xs, ws):
        k = x.shape[1]
        assert w.shape == (k, n) and x.shape == (m, k)
        in_specs += [pl.BlockSpec((tm, k), lambda i, j: (i, 0)), pl.BlockSpec((k, tn), lambda i, j: (0, j))]
        args += [x, w]
        blocks += 2 * _nbytes((tm, k), x.dtype) + 2 * _nbytes((k, tn), w.dtype)
    for kind, arr in extras:
        if kind == "row":
            assert arr.shape == (1, n)
            in_specs.append(pl.BlockSpec((1, tn), lambda i, j: (0, j)))
        else:
            assert kind == "full" and arr.shape == (m, n)
            in_specs.append(pl.BlockSpec((tm, tn), lambda i, j: (i, j)))
            blocks += 2 * _nbytes((tm, tn), arr.dtype)
        args.append(arr)
    blocks += 2 * _nbytes((tm, tn), out_dtype) + 2 * _nbytes((tm, tn), F32)
    if head_major:
        out_spec = pl.BlockSpec((tn // HEAD_DIM, tm, HEAD_DIM), lambda i, j: (j, i, 0))
        out_shape = jax.ShapeDtypeStruct((n // HEAD_DIM, m, HEAD_DIM), out_dtype)
    else:
        out_spec = pl.BlockSpec((tm, tn), lambda i, j: (i, j))
        out_shape = jax.ShapeDtypeStruct((m, n), out_dtype)
    return pl.pallas_call(
        functools.partial(_matmul_kernel, n_pairs=len(xs), n_extras=len(extras), epilogue=epilogue,
                          head_major=head_major),
        grid=(m // tm, n // tn),
        in_specs=in_specs,
        out_specs=out_spec,
        out_shape=out_shape,
        compiler_params=_params(("parallel", "parallel"), blocks),
        name=name,
    )(*args)


def _sgu_kernel(u_ref, vs_ref, gate_ref, w_ref, mask_ref, b_ref, o_ref, *, n_chunks):
    mask = mask_ref[...]
    for g in range(SGU_GROUPS):
        cols = slice(g * CHUNK, (g + 1) * CHUNK)
        wm = (w_ref[g] * mask).astype(BF16)
        for c in range(n_chunks):
            rows = slice(c * CHUNK, (c + 1) * CHUNK)
            mixed = jnp.dot(wm, vs_ref[rows, cols].astype(BF16), preferred_element_type=F32) + b_ref[:, cols]
            ya = u_ref[rows, cols].astype(F32) * mixed
            o_ref[rows, cols] = (gate_ref[rows, cols].astype(F32) * ya).astype(o_ref.dtype)


def _sgu(u, vs, gates, w_mix, mask, bias_full, *, n_chunks, name):
    t, width = u.shape
    rows = n_chunks * CHUNK
    assert t % rows == 0 and width == SGU_GROUPS * CHUNK
    blocks = (2 * _nbytes((rows, width), u.dtype) + 2 * _nbytes((rows, width), vs.dtype)
              + 2 * _nbytes((rows, width), gates.dtype) + 2 * _nbytes((rows, width), BF16)
              + 2 * _nbytes(w_mix.shape, F32) + 4 * _nbytes((CHUNK, width), F32))
    return pl.pallas_call(
        functools.partial(_sgu_kernel, n_chunks=n_chunks),
        grid=(t // rows,),
        in_specs=[
            pl.BlockSpec((rows, width), lambda i: (i, 0)),
            pl.BlockSpec((rows, width), lambda i: (i, 0)),
            pl.BlockSpec((rows, width), lambda i: (i, 0)),
            pl.BlockSpec((SGU_GROUPS, CHUNK, CHUNK), lambda i: (0, 0, 0)),
            pl.BlockSpec((CHUNK, CHUNK), lambda i: (0, 0)),
            pl.BlockSpec((CHUNK, width), lambda i: (0, 0)),
        ],
        out_specs=pl.BlockSpec((rows, width), lambda i: (i, 0)),
        out_shape=jax.ShapeDtypeStruct((t, width), BF16),
        compiler_params=_params(("parallel",), blocks),
        name=name,
    )(u, vs, gates, w_mix, mask, bias_full)


ATTN_ROWS = DIL_PAIRS[-1][0]
ATTN_UNROLL = 8


def _dil_attn_kernel(q0_ref, q1_ref, q2_ref, kc_ref, vc_ref, kp_ref, vp_ref, bias_ref, gate_ref, o_ref,
                     k_scr, v_scr, og0, og1, og2, lg0, lg1, lg2):
    rows = ATTN_ROWS
    first = pl.program_id(1) == 0
    k_scr[0:rows, :] = kp_ref[...]
    k_scr[rows:2 * rows, :] = kc_ref[...]
    v_scr[0:rows, :] = vp_ref[...]
    v_scr[rows:2 * rows, :] = vc_ref[...]
    prev_cols = (lax.broadcasted_iota(jnp.int32, (1, 2 * STEPS), 1) < STEPS).astype(F32)
    ones = jnp.ones((2 * STEPS, HEAD_DIM), BF16)
    scale = HEAD_DIM ** -0.5
    for g, (q_ref, og, lg) in enumerate(((q0_ref, og0, lg0), (q1_ref, og1, lg1), (q2_ref, og2, lg2))):
        dil = DIL_PAIRS[g][1]
        span = STEPS * dil
        bias = bias_ref[g]

        def block(u, carry, q_ref=q_ref, og=og, lg=lg, dil=dil, span=span, bias=bias):
            sp = u // dil
            base = sp * span + (u - sp * dil)
            q = q_ref[pl.ds(base, STEPS, stride=dil), :].astype(BF16)
            kk = k_scr[pl.ds(rows + base - span, 2 * STEPS, stride=dil), :].astype(BF16)
            vv = v_scr[pl.ds(rows + base - span, 2 * STEPS, stride=dil), :].astype(BF16)
            no_prev = jnp.where(jnp.logical_and(first, sp == 0), NEG_INF, 0.0).astype(F32)
            logits = (lax.dot_general(q, kk, (((1,), (1,)), ((), ())), preferred_element_type=F32) * scale
                      + bias + no_prev * prev_cols)
            m = jnp.max(logits, axis=-1, keepdims=True)
            e = jnp.exp(logits - m).astype(BF16)
            os_ = jnp.dot(e, jnp.concatenate([vv, ones], axis=1), preferred_element_type=F32)
            ssum = os_[:, HEAD_DIM:]
            og[pl.ds(base, STEPS, stride=dil), :] = os_[:, :HEAD_DIM] / ssum
            lg[pl.ds(base, STEPS, stride=dil), :] = m + jnp.log(ssum)
            return carry

        lax.fori_loop(0, rows // STEPS, block, 0, unroll=ATTN_UNROLL)

    def combine(c, carry):
        rs = pl.ds(pl.multiple_of(c * 2 * STEPS, 2 * STEPS), 2 * STEPS)
        l0, l1, l2 = lg0[rs, :], lg1[rs, :], lg2[rs, :]
        mx = jnp.maximum(jnp.maximum(l0, l1), l2)
        w0, w1, w2 = jnp.exp(l0 - mx), jnp.exp(l1 - mx), jnp.exp(l2 - mx)
        mix = (w0 * og0[rs, :] + w1 * og1[rs, :] + w2 * og2[rs, :]) / (w0 + w1 + w2)
        o_ref[rs, :] = (gate_ref[rs, :].astype(F32) * mix).astype(o_ref.dtype)
        return carry

    lax.fori_loop(0, rows // (2 * STEPS), combine, 0)


def _dil_attn(q, kv, bias_tabs, gates, *, name):
    _, s, _ = q.shape
    rows = ATTN_ROWS
    assert s % rows == 0
    slab = lambda off: pl.BlockSpec((None, rows, HEAD_DIM), lambda h, n, off=off: (off + h, n, 0))
    prev = lambda off: pl.BlockSpec((None, rows, HEAD_DIM),
                                    lambda h, n, off=off: (off + h, jnp.maximum(n - 1, 0), 0))
    blocks = (2 * 7 * _nbytes((rows, HEAD_DIM), F32) + 4 * _nbytes((rows, HEAD_DIM), gates.dtype)
              + 2 * _nbytes((N_DIL, STEPS, 2 * STEPS), F32) + 10 * _nbytes((rows, HEAD_DIM), F32))
    return pl.pallas_call(
        _dil_attn_kernel,
        grid=(KV_HEADS, s // rows),
        in_specs=[
            slab(0), slab(KV_HEADS), slab(2 * KV_HEADS),
            slab(0), slab(KV_HEADS),
            prev(0), prev(KV_HEADS),
            pl.BlockSpec((N_DIL, None, STEPS, 2 * STEPS), lambda h, n: (0, h, 0, 0)),
            pl.BlockSpec((rows, HEAD_DIM), lambda h, n: (n, KV_HEADS + h)),
        ],
        out_specs=pl.BlockSpec((rows, HEAD_DIM), lambda h, n: (n, h)),
        out_shape=jax.ShapeDtypeStruct((s, KV_HEADS * HEAD_DIM), BF16),
        scratch_shapes=[pltpu.VMEM((2 * rows, HEAD_DIM), F32), pltpu.VMEM((2 * rows, HEAD_DIM), F32)]
        + [pltpu.VMEM((rows, HEAD_DIM), F32)] * 6,
        compiler_params=_params(("parallel", "parallel"), blocks),
        name=name,
    )(q, q, q, kv, kv, kv, kv, bias_tabs, gates)


def _smp_attn_kernel(q_ref, kvn_ref, cache_ref, tab_ref, gate_ref, o_ref, *, n_past, n_new, n_keys):
    pad = jnp.zeros((n_keys - n_past - n_new, HEAD_DIM), F32)
    rows_per_pos = 2 * KV_HEADS

    def head(h, carry):
        keys = jnp.concatenate([cache_ref[pl.ds(h, n_past, stride=rows_per_pos), :], kvn_ref[h], pad], axis=0)
        vals = jnp.concatenate([cache_ref[pl.ds(KV_HEADS + h, n_past, stride=rows_per_pos), :],
                                kvn_ref[KV_HEADS + h], pad], axis=0)
        q = jnp.concatenate([q_ref[h], q_ref[KV_HEADS + h], q_ref[2 * KV_HEADS + h],
                             jnp.zeros((n_new, HEAD_DIM), F32)], axis=0)
        logits = lax.dot_general(q.astype(BF16), keys.astype(BF16), (((1,), (1,)), ((), ())),
                                 preferred_element_type=F32) * (HEAD_DIM ** -0.5) + tab_ref[h]
        m = jnp.max(logits, axis=-1, keepdims=True)
        e = jnp.exp(logits - m)
        s = jnp.sum(e, axis=-1, keepdims=True)
        o = jnp.dot(e.astype(BF16), vals.astype(BF16), preferred_element_type=F32) / s
        lse = m + jnp.log(s)
        ls = [lse[g * n_new:(g + 1) * n_new] for g in range(N_DIL)]
        mx = jnp.maximum(jnp.maximum(ls[0], ls[1]), ls[2])
        ws = [jnp.exp(l - mx) for l in ls]
        mix = sum(ws[g] * o[g * n_new:(g + 1) * n_new] for g in range(N_DIL)) / (ws[0] + ws[1] + ws[2])
        cols = pl.ds(pl.multiple_of(h * HEAD_DIM, HEAD_DIM), HEAD_DIM)
        o_ref[:, cols] = (gate_ref[:, cols].astype(F32) * mix).astype(o_ref.dtype)
        return carry

    lax.fori_loop(0, KV_HEADS, head, 0)


def _smp_attn(q, kv, cache, tab, gates, *, n_new, out_dtype, name):
    b, cache_rows, _ = cache.shape
    n_past = cache_rows // (2 * KV_HEADS)
    n_keys = tab.shape[-1]
    w = KV_HEADS * HEAD_DIM
    blocks = (2 * _nbytes((cache_rows, HEAD_DIM), F32) + 2 * _nbytes((n_keys, HEAD_DIM), F32)
              + 2 * _nbytes(tab.shape, F32) + 8 * _nbytes(((N_DIL + 1) * n_new, n_keys), F32))
    return pl.pallas_call(
        functools.partial(_smp_attn_kernel, n_past=n_past, n_new=n_new, n_keys=n_keys),
        grid=(b,),
        in_specs=[
            pl.BlockSpec((N_DIL * KV_HEADS, n_new, HEAD_DIM), lambda i: (0, i, 0)),
            pl.BlockSpec((2 * KV_HEADS, n_new, HEAD_DIM), lambda i: (0, i, 0)),
            pl.BlockSpec((None, cache_rows, HEAD_DIM), lambda i: (i, 0, 0)),
            pl.BlockSpec(tab.shape, lambda i: (0, 0, 0)),
            pl.BlockSpec((n_new, w), lambda i: (i, 1)),
        ],
        out_specs=pl.BlockSpec((n_new, w), lambda i: (i, 0)),
        out_shape=jax.ShapeDtypeStruct((b * n_new, w), out_dtype),
        compiler_params=_params(("parallel",), blocks),
        name=name,
    )(q, kv, cache, tab, gates)


def _mem_attn_kernel(q_ref, k_ref, v_ref, o_ref, *, head_dim):
    scale = head_dim ** -0.5
    for h in range(q_ref.shape[1] // head_dim):
        cs = slice(h * head_dim, (h + 1) * head_dim)
        logits = lax.dot_general(q_ref[:, cs].astype(BF16), k_ref[:, cs].astype(BF16), (((1,), (1,)), ((), ())),
                                 preferred_element_type=F32) * scale
        m = jnp.max(logits, axis=-1, keepdims=True)
        e = jnp.exp(logits - m)
        s = jnp.sum(e, axis=-1, keepdims=True)
        o = jnp.dot(e.astype(BF16), v_ref[:, cs].astype(BF16), preferred_element_type=F32)
        o_ref[:, cs] = (o / s).astype(o_ref.dtype)


def _mem_attn(q, mem_kv, *, tm, per_batch, out_dtype, name):
    t, w = q.shape
    m_tok = mem_kv.shape[-2]
    if per_batch:
        kspec = pl.BlockSpec((None, m_tok, w), lambda i: (i, 0, 0))
        vspec = pl.BlockSpec((None, m_tok, w), lambda i: (i, 0, 1))
    else:
        kspec = pl.BlockSpec((m_tok, w), lambda i: (0, 0))
        vspec = pl.BlockSpec((m_tok, w), lambda i: (0, 1))
    blocks = 4 * _nbytes((tm, w), BF16) + 4 * _nbytes((m_tok, w), F32) + 6 * _nbytes((tm, m_tok), F32) \
        + 2 * _nbytes((tm, w), F32)
    return pl.pallas_call(
        functools.partial(_mem_attn_kernel, head_dim=w // MEM_HEADS),
        grid=(t // tm,),
        in_specs=[pl.BlockSpec((tm, w), lambda i: (i, 0)), kspec, vspec],
        out_specs=pl.BlockSpec((tm, w), lambda i: (i, 0)),
        out_shape=jax.ShapeDtypeStruct((t, w), out_dtype),
        compiler_params=_params(("parallel",), blocks),
        name=name,
    )(q, mem_kv, mem_kv)


def _oddeven_merge_sort_pairs(n):
    pairs = []

    def merge(lo, hi, r):
        step = r * 2
        if step < hi - lo:
            merge(lo, hi, step)
            merge(lo + r, hi, step)
            pairs.extend((i, i + r) for i in range(lo + r, hi - r, step))
        else:
            pairs.append((lo, lo + r))

    def sort(lo, hi):
        if hi - lo >= 1:
            mid = lo + (hi - lo) // 2
            sort(lo, mid)
            sort(mid + 1, hi)
            merge(lo, hi, 1)

    sort(0, n - 1)
    return pairs


def _compare_exchange(wires, i, j):
    a, b = wires[i], wires[j]
    if b is None:
        return
    if a is None:
        wires[i], wires[j] = b, None
        return
    wires[i], wires[j] = jnp.maximum(a, b), jnp.minimum(a, b)


def _sort_desc(wires):
    for i, j in _oddeven_merge_sort_pairs(len(wires)):
        _compare_exchange(wires, i, j)
    return wires


def _bitonic_merge_desc(wires):
    n = len(wires)
    dist = n // 2
    while dist >= 1:
        for i in range(n):
            if i & dist == 0:
                _compare_exchange(wires, i, i + dist)
        dist //= 2
    return wires


def _top16_desc(s):
    wires = [s[SUBLANES * i:SUBLANES * (i + 1), :] for i in range(PEER_N_KEYS // SUBLANES)]
    wires = _sort_desc(wires)
    for shift in (4, 2, 1):
        wires = [jnp.maximum(wires[i], pltpu.roll(wires[PEER_TOPK - 1 - i], shift, 0)) for i in range(PEER_TOPK)]
        wires = _bitonic_merge_desc(wires)
    return wires


_CAND_PAIRS = [(j, k) for j in range(PEER_TOPK) for k in range(PEER_TOPK) if (j + 1) * (k + 1) <= PEER_TOPK]


def _peer_scores_kernel(qt_ref, k1_ref, k2_ref, cnt_ref, e1_ref, rank_ref, e2_ref, s1_ref, s2_ref,
                        *, n_lane_tiles):
    for h in range(PEER_HEADS):
        r0 = h * 2 * PEER_KEY_DIM
        s1_ref[h] = jnp.dot(k1_ref[h], qt_ref[r0:r0 + PEER_KEY_DIM, :], preferred_element_type=F32)
        s2_ref[h] = jnp.dot(k2_ref[h], qt_ref[r0 + PEER_KEY_DIM:r0 + 2 * PEER_KEY_DIM, :],
                            preferred_element_type=F32)
    sub = lax.broadcasted_iota(jnp.int32, (SUBLANES, LANES), 0)

    def lane_tile(lt, carry):
        lanes = pl.ds(pl.multiple_of(lt * LANES, LANES), LANES)
        a_pack = [None] * PEER_TOPK
        b_pack = [None] * PEER_TOPK
        for h in range(PEER_HEADS):
            ta = _top16_desc(s1_ref[h, :, lanes])
            tb = _top16_desc(s2_ref[h, :, lanes])
            for j in range(PEER_TOPK):
                a_pack[j] = ta[j] if h == 0 else jnp.where(sub == h, ta[j], a_pack[j])
                b_pack[j] = tb[j] if h == 0 else jnp.where(sub == h, tb[j], b_pack[j])
        cands = [a_pack[j] + b_pack[k] for j, k in _CAND_PAIRS]
        top = cands[0]
        wires = _sort_desc(list(cands) + [None] * (64 - len(cands)))
        tau = wires[PEER_TOPK - 1]
        z = jnp.zeros_like(tau)
        for c in cands:
            z = z + jnp.where(c >= tau, jnp.exp(c - top), 0.0)
        zinv = 1.0 / z
        for h in range(PEER_HEADS):
            s1 = s1_ref[h, :, lanes]
            s2 = s2_ref[h, :, lanes]
            tau_h = tau[h:h + 1, :]
            cnt = jnp.zeros_like(s1)
            rank = jnp.zeros_like(s2)
            for k in range(PEER_TOPK):
                b_k = b_pack[k][h:h + 1, :]
                cnt = jnp.where((s1 + b_k) >= tau_h, k + 1.0, cnt)
                rank = jnp.where(b_k > s2, k + 1.0, rank)
            cnt_ref[h, :, lanes] = cnt
            rank_ref[h, :, lanes] = rank.astype(rank_ref.dtype)
            e1_ref[h, :, lanes] = jnp.exp(s1 - a_pack[0][h:h + 1, :])
            e2_ref[h, :, lanes] = (jnp.exp(s2 - b_pack[0][h:h + 1, :]) * zinv[h:h + 1, :]).astype(e2_ref.dtype)
        return carry

    lax.fori_loop(0, n_lane_tiles, lane_tile, 0)


def _peer_scores(qt, keys1, keys2, *, tl, name):
    d, t = qt.shape
    assert t % tl == 0 and d == PEER_HEADS * 2 * PEER_KEY_DIM
    first = jax.ShapeDtypeStruct((PEER_HEADS, PEER_N_KEYS, t), F32)
    second = jax.ShapeDtypeStruct((PEER_HEADS, PEER_N_KEYS, t), BF16)
    bspec = lambda: pl.BlockSpec((PEER_HEADS, PEER_N_KEYS, tl), lambda i: (0, 0, i))
    blocks = 2 * _nbytes((d, tl), BF16) + 4 * _nbytes(keys1.shape, BF16) + 8 * _nbytes((PEER_HEADS, PEER_N_KEYS, tl), F32)
    return pl.pallas_call(
        functools.partial(_peer_scores_kernel, n_lane_tiles=tl // LANES),
        grid=(t // tl,),
        in_specs=[
            pl.BlockSpec((d, tl), lambda i: (0, i)),
            pl.BlockSpec(keys1.shape, lambda i: (0, 0, 0)),
            pl.BlockSpec(keys2.shape, lambda i: (0, 0, 0)),
        ],
        out_specs=[bspec(), bspec(), bspec(), bspec()],
        out_shape=[first, first, second, second],
        scratch_shapes=[pltpu.VMEM((PEER_HEADS, PEER_N_KEYS, tl), F32)] * 2,
        compiler_params=_params(("parallel",), blocks),
        name=name,
    )(qt, keys1, keys2)


def _peer_dense_step(u_ref, v_ref, ht_ref, cnt_ref, e1_ref, rank_ref, e2_ref, o_ref,
                     a_new, a_old, w_new, w_old, key_row_in_group, rows_per_block, n_lane_tiles,
                     activate=True, gate=True, accumulate=True):
    half = PEER_N_KEYS // 2
    d = o_ref.shape[1]
    bf16_rows = 2 * SUBLANES

    def gate_lane_tile(lt):
        lanes = slice(lt * LANES, (lt + 1) * LANES)
        for i0 in range(0, rows_per_block, 2):
            gates = {}
            for h in range(PEER_HEADS):
                cnt_rows = cnt_ref[h, :, lanes]
                e1_rows = e1_ref[h, :, lanes]
                rank = rank_ref[h, :, lanes]
                e2t = e2_ref[h, :, lanes]
                for ii in (i0, i0 + 1):
                    r = key_row_in_group + ii
                    cnt_b = jnp.broadcast_to(cnt_rows[r:r + 1, :], (PEER_N_KEYS, LANES)).astype(BF16)
                    e1_b = jnp.broadcast_to(e1_rows[r:r + 1, :], (PEER_N_KEYS, LANES)).astype(BF16)
                    hit = jnp.minimum(jnp.maximum(cnt_b - rank, 0), 1)
                    term = (hit * e1_b) * e2t
                    gates[ii] = term if h == 0 else gates[ii] + term
            for ii in (i0, i0 + 1):
                erows = slice(ii * PEER_N_KEYS, (ii + 1) * PEER_N_KEYS)
                tile = gates[ii].astype(F32) * jax.nn.gelu(a_old[erows, lanes])
                w_new[lanes, erows] = tile.T.astype(w_new.dtype)

    n_slices = n_lane_tiles // 2
    for c in range(n_slices):
        lanes2 = slice(2 * c * LANES, 2 * (c + 1) * LANES)
        cols = slice(c * d // n_slices, (c + 1) * d // n_slices)
        if activate:
            a_new[:, lanes2] = jnp.dot(u_ref[...], ht_ref[:, lanes2], preferred_element_type=F32)
        if gate:
            gate_lane_tile(2 * c)
        if accumulate:
            o_ref[:, cols] += jnp.dot(w_old[...], v_ref[:, cols], preferred_element_type=F32)
        if gate:
            gate_lane_tile(2 * c + 1)


def _peer_dense_kernel(u_ref, v_ref, ht_ref, cnt_ref, e1_ref, rank_ref, e2_ref, o_ref,
                       a0, a1, w0, w1, *, rows_per_block, n_lane_tiles, n_blocks):
    j = pl.program_id(1)

    @pl.when(j == 0)
    def _():
        o_ref[...] = jnp.zeros_like(o_ref)

    blocks_per_group = SUBLANES // rows_per_block
    step = functools.partial(_peer_dense_step, u_ref, v_ref, ht_ref, cnt_ref, e1_ref, rank_ref, e2_ref,
                             o_ref, rows_per_block=rows_per_block, n_lane_tiles=n_lane_tiles)
    even_step = functools.partial(step, a_new=a0, a_old=a1, w_new=w1, w_old=w0,
                                  key_row_in_group=(blocks_per_group - 1) * rows_per_block)
    odd_step = functools.partial(step, a_new=a1, a_old=a0, w_new=w0, w_old=w1, key_row_in_group=0)
    even = lax.rem(j, 2) == 0
    filling = j < 2
    draining = j >= n_blocks
    steady = jnp.logical_not(jnp.logical_or(filling, draining))

    pl.when(jnp.logical_and(even, steady))(even_step)
    pl.when(jnp.logical_and(jnp.logical_not(even), steady))(odd_step)
    pl.when(j == 0)(functools.partial(even_step, gate=False, accumulate=False))
    pl.when(j == 1)(functools.partial(odd_step, accumulate=False))
    pl.when(j == n_blocks)(functools.partial(even_step, activate=False))
    pl.when(j == n_blocks + 1)(functools.partial(odd_step, activate=False, gate=False))


def _peer_dense(u_tab, v_tab, ht, cnt, e1, rank, e2, *, tm, eb, name):
    n_exp, d = u_tab.shape
    t = ht.shape[1]
    assert t % tm == 0 and tm % (2 * LANES) == 0 and n_exp % eb == 0 and eb % PEER_N_KEYS == 0
    assert eb // PEER_N_KEYS in (SUBLANES // 2, SUBLANES)
    n_blocks = n_exp // eb
    assert n_blocks % 2 == 0
    last = n_blocks - 1
    full = lambda: pl.BlockSpec((PEER_HEADS, PEER_N_KEYS, tm), lambda i, j: (0, 0, i))
    blocks_per_group = SUBLANES * PEER_N_KEYS // eb
    group = lambda: pl.BlockSpec((PEER_HEADS, SUBLANES, tm),
                                 lambda i, j: (0, jnp.clip(j - 1, 0, last) // blocks_per_group, i))
    blocks = (2 * _nbytes((eb, d), BF16) * 2 + 2 * _nbytes((d, tm), BF16)
              + 4 * _nbytes((PEER_HEADS, PEER_N_KEYS, tm), BF16) + 4 * _nbytes((PEER_HEADS, SUBLANES, tm), F32)
              + 2 * _nbytes((d, tm), F32) + 2 * _nbytes((eb, tm), F32) + 2 * _nbytes((eb, tm), BF16)
              + 2 * _nbytes((d, tm), F32))
    return pl.pallas_call(
        functools.partial(_peer_dense_kernel, rows_per_block=eb // PEER_N_KEYS, n_lane_tiles=tm // LANES,
                          n_blocks=n_blocks),
        grid=(t // tm, n_blocks + 2),
        in_specs=[
            pl.BlockSpec((eb, d), lambda i, j: (jnp.minimum(j, last), 0)),
            pl.BlockSpec((eb, d), lambda i, j: (jnp.clip(j - 2, 0, last), 0)),
            pl.BlockSpec((d, tm), lambda i, j: (0, i)),
            group(), group(), full(), full(),
        ],
        out_specs=pl.BlockSpec((tm, d), lambda i, j: (i, 0)),
        out_shape=jax.ShapeDtypeStruct((t, d), F32),
        scratch_shapes=[pltpu.VMEM((eb, tm), F32), pltpu.VMEM((eb, tm), F32),
                        pltpu.VMEM((tm, eb), BF16), pltpu.VMEM((tm, eb), BF16)],
        compiler_params=_params(("parallel", "arbitrary"), blocks),
        name=name,
    )(u_tab, v_tab, ht, cnt, e1, rank, e2)


def _final_kernel(x_ref, y_ref, g_ref, o_ref):
    x = x_ref[...] + y_ref[...]
    o_ref[...] = (x * lax.rsqrt(jnp.mean(x * x, axis=-1, keepdims=True) + EPS) * g_ref[...]).astype(o_ref.dtype)


def _final(x, y, g, *, tm, name):
    t, d = x.shape
    blocks = 6 * _nbytes((tm, d), F32) + 2 * _nbytes((tm, d), F32)
    row = lambda: pl.BlockSpec((tm, d), lambda i: (i, 0))
    return pl.pallas_call(
        _final_kernel,
        grid=(t // tm,),
        in_specs=[row(), row(), pl.BlockSpec((1, d), lambda i: (0, 0))],
        out_specs=row(),
        out_shape=jax.ShapeDtypeStruct((t, d), F32),
        compiler_params=_params(("parallel",), blocks),
        name=name,
    )(x, y, g.reshape(1, d).astype(F32))


def _t5_bucket(dist):
    max_exact = REL_BUCKETS // 2
    d = np.maximum(dist, 1).astype(np.float64)
    large = max_exact + (np.log(d / max_exact) / math.log(REL_MAX_DIST / max_exact)
                         * (REL_BUCKETS - max_exact)).astype(np.int64)
    large = np.minimum(large, REL_BUCKETS - 1)
    return np.where(dist < max_exact, dist, large).astype(np.int32)


def _spread_bias(b):
    return jnp.broadcast_to(b.T[:, :, None], (CHUNK, SGU_GROUPS, CHUNK)).reshape(CHUNK, SGU_GROUPS * CHUNK)


def _group_bias(rel_bias, group, dil):
    bucket = _t5_bucket(dil * np.arange(STEPS + 1))
    table = rel_bias[:, group * KV_HEADS:(group + 1) * KV_HEADS].astype(F32)
    runs, start = [], 0
    for i in range(1, len(bucket) + 1):
        if i == len(bucket) or bucket[i] != bucket[start]:
            runs.append(jnp.broadcast_to(table[bucket[start]:bucket[start] + 1], (i - start, KV_HEADS)))
            start = i
    return jnp.concatenate(runs, axis=0).T


def _prompt_bias_table(rel_bias, group, dil):
    period = 3 * STEPS
    bias = _group_bias(rel_bias, group, dil)
    vec = jnp.concatenate([bias[:, ::-1], jnp.full((KV_HEADS, period - STEPS - 1), NEG_INF, F32)], axis=1)
    flat = jnp.broadcast_to(vec[:, None, :], (KV_HEADS, STEPS, period)).reshape(KV_HEADS, STEPS * period)
    skew = flat[:, :STEPS * (period - 1)].reshape(KV_HEADS, STEPS, period - 1)
    return skew[:, :, :2 * STEPS]


def _sample_bias_table(rel_bias, n_past, n_new, n_keys):
    rows = []
    for g, (win, dil) in enumerate(DIL_PAIRS):
        rev = _group_bias(rel_bias, g, dil)[:, ::-1]
        for t in range(n_new):
            lo = n_past + t - win
            hi = n_keys - (n_past + t + 1)
            rows.append(lax.pad(rev, jnp.asarray(NEG_INF, F32), [(0, 0, 0), (lo, hi, dil - 1)]))
    rows += [jnp.zeros((KV_HEADS, n_keys), F32)] * n_new
    return jnp.stack(rows, axis=1)


def _input_projection(x, p, *, tm, norm_tm, act_dtype, tag):
    width = SGU_GROUPS * CHUNK
    hn = _rmsnorm(x, p["norm_mix"], tm=norm_tm, out_dtype=BF16, name=f"{tag}_norm_mix")
    mm = functools.partial(_matmul, [hn], tm=tm, tn=width)
    u = mm([p["w_u"]], out_dtype=BF16, epilogue=_epi_gelu, name=f"{tag}_proj_u")
    vs = mm([p["w_vs"]], out_dtype=F32, epilogue=_epi_gelu_layernorm,
            extras=(("row", p["sgu_ln_g"]), ("row", p["sgu_ln_b"])), name=f"{tag}_proj_vs")
    q = mm([p["w_q"]], out_dtype=F32, head_major=True, name=f"{tag}_proj_q")
    kv = mm([p["w_kv"]], out_dtype=F32, head_major=True, name=f"{tag}_proj_kv")
    gates = mm([p["w_gates"]], out_dtype=act_dtype, epilogue=_epi_sigmoid, name=f"{tag}_proj_gates")
    return u, vs, q, kv, gates


def _channel_stages(x, ya, yb, mem_kv, p, *, tm, norm_tm, tl, peer_tm, per_batch, mem_tm, act_dtype, tag):
    d = x.shape[1]
    x1 = _matmul([ya, yb], [p["w_out_a"], p["w_out_b"]], tm=tm, tn=d // 2, out_dtype=F32,
                 epilogue=_epi_residual, extras=(("full", x),), name=f"{tag}_proj_out")
    hq = _rmsnorm(x1, p["norm_mem"], tm=norm_tm, out_dtype=BF16, name=f"{tag}_norm_mem")
    qm = _matmul([hq], [p["w_mq"]], tm=tm, tn=d // 2, out_dtype=act_dtype, name=f"{tag}_mem_q")
    om = _mem_attn(qm, mem_kv, tm=mem_tm, per_batch=per_batch, out_dtype=act_dtype, name=f"{tag}_mem_attn")
    x2 = _matmul([om], [p["w_mo"]], tm=tm, tn=d // 2, out_dtype=F32, epilogue=_epi_residual,
                 extras=(("full", x1),), name=f"{tag}_mem_o")
    ht = _rmsnorm(x2, p["norm_peer"], tm=tl, out_dtype=BF16, transpose=True, name=f"{tag}_norm_peer")
    qt = _matmul([p["peer_wq_t"]], [ht], tm=d // 2, tn=tl, out_dtype=BF16, name=f"{tag}_peer_q")
    cnt, e1, rank, e2 = _peer_scores(qt, p["peer_keys1"], p["peer_keys2"], tl=tl, name=f"{tag}_peer_scores")
    y = _peer_dense(p["peer_u"], p["peer_v"], ht, cnt, e1, rank, e2, tm=peer_tm, eb=512,
                    name=f"{tag}_peer_dense")
    return _final(x2, y, p["norm_final"], tm=tl, name=f"{tag}_final")


def kernel(x_prompt, x_sample, mem_prompt, cache_win, cache_mem_kv, rel_bias, norm_mix, w_in, sgu_ln_g, sgu_ln_b, sgu_w, sgu_b, w_out, norm_mem, norm_memtok, w_mq, w_mk, w_mv, w_mo, norm_peer, peer_wq, peer_keys1, peer_keys2, peer_u, peer_v, norm_final):
    depth = norm_mix.shape[0]
    batch, seq, d = x_prompt.shape
    dec_batch, dec_seq, _ = x_sample.shape
    assert depth == 1 and batch == 1, "one layer and one prompt sequence are supported"
    width = SGU_GROUPS * CHUNK
    n_past = cache_win.shape[2]
    assert seq % (DIL_PAIRS[-1][0]) == 0 and dec_seq <= CHUNK and CHUNK % dec_seq == 0

    w_in_b = w_in[0].astype(BF16)
    o_q = 2 * width
    o_k = o_q + N_DIL * width
    o_g = o_k + 2 * width
    p = dict(
        norm_mix=norm_mix[0], norm_mem=norm_mem[0], norm_peer=norm_peer[0], norm_final=norm_final,
        w_u=w_in_b[:, :width], w_vs=w_in_b[:, width:o_q], w_q=w_in_b[:, o_q:o_k], w_kv=w_in_b[:, o_k:o_g],
        w_gates=w_in_b[:, o_g:],
        sgu_ln_g=sgu_ln_g[0].reshape(1, width), sgu_ln_b=sgu_ln_b[0].reshape(1, width),
        w_out_a=w_out[0, :width].astype(BF16), w_out_b=w_out[0, width:].astype(BF16),
        w_mq=w_mq[0].astype(BF16), w_mo=w_mo[0].astype(BF16),
        peer_wq_t=peer_wq[0].T.astype(BF16),
        peer_keys1=peer_keys1[0].astype(BF16), peer_keys2=peer_keys2[0].astype(BF16),
        peer_u=peer_u[0].astype(BF16), peer_v=peer_v[0].astype(BF16),
    )

    xp = x_prompt.reshape(seq, d)
    u, vs, q, kv, gates = _input_projection(xp, p, tm=PROMPT_TM, norm_tm=NORM_TM, act_dtype=BF16, tag="p")
    tril = np.tril(np.ones((CHUNK, CHUNK), np.float32))
    bias_full = _spread_bias(sgu_b[0])
    ya = _sgu(u, vs, gates, sgu_w[0], jnp.asarray(tril), bias_full, n_chunks=4, name="p_sgu")
    bias_tabs = jnp.stack([_prompt_bias_table(rel_bias, g, dil) for g, (_, dil) in enumerate(DIL_PAIRS)])
    yb = _dil_attn(q, kv, bias_tabs, gates, name="p_dil_attn")
    mem_n = _rmsnorm(mem_prompt.reshape(-1, d), norm_memtok[0], tm=mem_prompt.shape[1], out_dtype=BF16,
                     name="p_norm_memtok")
    w_mkv = jnp.concatenate([w_mk[0], w_mv[0]], axis=1).astype(BF16)
    mem_kv = _matmul([mem_n], [w_mkv], tm=mem_n.shape[0], tn=d // 2, out_dtype=F32, name="p_mem_kv")
    y_prompt = _channel_stages(xp, ya, yb, mem_kv, p, tm=PROMPT_TM, norm_tm=NORM_TM, tl=PEER_TOKEN_TILE,
                               peer_tm=PEER_TOKEN_TILE, per_batch=False, mem_tm=PEER_TOKEN_TILE,
                               act_dtype=BF16, tag="p")
    win_rows = min(DIL_PAIRS[-1][0], seq)
    state_win_p = kv[:, seq - win_rows:].transpose(1, 0, 2).reshape(1, 1, win_rows, 2, KV_HEADS, HEAD_DIM)
    state_mem_p = mem_kv.reshape(1, 1, mem_kv.shape[0], 2, MEM_HEADS, d // MEM_HEADS)

    ts = dec_batch * dec_seq
    assert ts % CHUNK == 0
    xs = x_sample.reshape(ts, d)
    u, vs, q, kv, gates = _input_projection(xs, p, tm=ts, norm_tm=ts, act_dtype=F32, tag="s")
    reps = CHUNK // dec_seq
    blockdiag = np.kron(np.eye(reps, dtype=np.float32), np.tril(np.ones((dec_seq, dec_seq), np.float32)))
    w_small = jnp.tile(sgu_w[0][:, :dec_seq, :dec_seq], (1, reps, reps))
    bias_small = _spread_bias(jnp.tile(sgu_b[0][:, :dec_seq], (1, reps)))
    ya = _sgu(u, vs, gates, w_small, jnp.asarray(blockdiag), bias_small, n_chunks=1, name="s_sgu")
    n_keys = -(-(n_past + dec_seq) // LANES) * LANES
    tab = _sample_bias_table(rel_bias, n_past, dec_seq, n_keys)
    cache = cache_win[0].reshape(dec_batch, n_past * 2 * KV_HEADS, HEAD_DIM)
    yb = _smp_attn(q, kv, cache, tab, gates, n_new=dec_seq, out_dtype=F32, name="s_smp_attn")
    mem_s = cache_mem_kv[0].reshape(dec_batch, cache_mem_kv.shape[2], 2 * d)
    y_sample = _channel_stages(xs, ya, yb, mem_s, p, tm=ts, norm_tm=ts, tl=ts, peer_tm=ts, per_batch=True,
                               mem_tm=dec_seq, act_dtype=F32, tag="s")
    state_win_s = kv.transpose(1, 0, 2).reshape(1, dec_batch, dec_seq, 2, KV_HEADS, HEAD_DIM)
    state_sgu_s = vs.reshape(1, dec_batch, dec_seq, width)

    return (y_prompt.reshape(batch, seq, d), y_sample.reshape(dec_batch, dec_seq, d),
            state_win_p, state_mem_p, state_win_s, state_sgu_s)
```
